```python
import jax, jax.numpy as jnp
from jax import lax
import numpy as np

D_MODEL = 1024
BATCH = 16
SEQ = 2048
DEPTH = 4
DEC_BATCH = 8
DEC_SEQ = 16
PAST_LEN = 2048

CHUNK = 64
N_MIXERS = 3
N_FOX = (DEPTH + 2) // 3
N_HGRN = (DEPTH + 1) // 3
N_POOL = DEPTH // 3
FOX_HEADS = 16
FOX_HEAD_DIM = D_MODEL // FOX_HEADS
Q_BLOCK = 128
HGRN_HEADS = 8
HGRN_DK = D_MODEL // HGRN_HEADS
HGRN_DV = D_MODEL // HGRN_HEADS
POOL_WINDOWS = (2, 4, 8, 16)
POOL_GROUP = D_MODEL // len(POOL_WINDOWS)
POOL_HIST = max(POOL_WINDOWS) - 1
FFN_HIDDEN = ((8 * D_MODEL + 3 * 256 - 1) // (3 * 256)) * 256
ALPHA = (2 * DEPTH) ** 0.25
BETA = (8 * DEPTH) ** -0.25
LN_EPS = 1e-5
RMS_EPS = 1e-6

kernel_name = 'hybrid_fox_hgrn2_pool_streaming_step'


def layer_norm(x, g, b):
    xf = x.astype(jnp.float32)
    mu = jnp.mean(xf, axis=-1, keepdims=True)
    var = jnp.mean(jnp.square(xf - mu), axis=-1, keepdims=True)
    return ((xf - mu) * lax.rsqrt(var + LN_EPS) * g + b).astype(x.dtype)


def swiglu_ffn(x, w_in, w_out):
    a, u = jnp.split(x @ w_in, 2, axis=-1)
    return (jax.nn.silu(a) * u) @ w_out


def fox_project(x, w_in, b_f):
    B, T, _ = x.shape
    proj = x @ w_in
    hs = (B, T, FOX_HEADS, FOX_HEAD_DIM)
    q = proj[..., :D_MODEL].reshape(hs)
    k = proj[..., D_MODEL:2 * D_MODEL].reshape(hs)
    v = proj[..., 2 * D_MODEL:3 * D_MODEL].reshape(hs)
    lf = jax.nn.log_sigmoid((proj[..., 3 * D_MODEL:] + b_f).astype(jnp.float32))
    return q, k, v, lf


def fox_attend(q, k, v, cq, ck, qpos, kpos):
    s = jnp.einsum('bqhd,bkhd->bhqk', q, k).astype(jnp.float32) * (FOX_HEAD_DIM ** -0.5)
    s = s + jnp.swapaxes(cq, 1, 2)[:, :, :, None] - jnp.swapaxes(ck, 1, 2)[:, :, None, :]
    s = jnp.where(kpos[None, :] <= qpos[:, None], s, -jnp.inf)
    p = jax.nn.softmax(s, axis=-1).astype(v.dtype)
    return jnp.einsum('bhqk,bkhd->bqhd', p, v)


def fox_prompt(x, w_in, b_f, w_out):
    B, T, _ = x.shape
    q, k, v, lf = fox_project(x, w_in, b_f)
    c = jnp.cumsum(lf, axis=1)
    pos = jnp.arange(T)
    blocks = []
    for lo in range(0, T, Q_BLOCK):
        hi = min(lo + Q_BLOCK, T)
        blocks.append(fox_attend(q[:, lo:hi], k[:, :hi], v[:, :hi], c[:, lo:hi], c[:, :hi], pos[lo:hi], pos[:hi]))
    o = jnp.concatenate(blocks, axis=1).reshape(B, T, D_MODEL)
    return o @ w_out, k, v, lf


def fox_sample(x, cache_k, cache_v, cache_lf, w_in, b_f, w_out):
    B, T, _ = x.shape
    P = cache_k.shape[1]
    q, k, v, lf = fox_project(x, w_in, b_f)
    k_all = jnp.concatenate([cache_k.astype(k.dtype), k], axis=1)
    v_all = jnp.concatenate([cache_v.astype(v.dtype), v], axis=1)
    c = jnp.cumsum(jnp.concatenate([cache_lf.astype(jnp.float32), lf], axis=1), axis=1)
    kpos = jnp.arange(P + T)
    o = fox_attend(q, k_all, v_all, c[:, P:], c, kpos[P:], kpos)
    return o.reshape(B, T, D_MODEL) @ w_out, k, v, lf


def hgrn_chunk_step(S, inp):
    q, k, v, g = inp
    L = q.shape[1]
    b = jnp.cumsum(g, axis=1)
    causal = jnp.arange(L)[:, None] >= jnp.arange(L)[None, :]
    diff = b[:, :, None] - b[:, None, :]
    decay = jnp.exp(jnp.where(causal[None, :, :, None, None], diff, -jnp.inf))
    scores = jnp.einsum('bthc,bshc,btshc->bhts', q, k, decay)
    o = jnp.einsum('bhts,bshv->bthv', scores, v) + jnp.einsum('bthc,bhcv->bthv', q * jnp.exp(b), S)
    b_last = b[:, -1]
    S_new = jnp.exp(b_last)[..., None] * S + jnp.einsum('bshc,bshv->bhcv', k * jnp.exp(b_last[:, None] - b), v)
    return S_new, o


def hgrn_mix(x, S0, w_in, lb, norm_g, w_out):
    B, T, _ = x.shape
    q, fx, iv, gate = jnp.split(x @ w_in, 4, axis=-1)
    ks = (B, T, HGRN_HEADS, HGRN_DK)
    q = jax.nn.silu(q.astype(jnp.float32)).reshape(ks)
    fx = fx.astype(jnp.float32).reshape(ks)
    lb = lb.reshape(HGRN_HEADS, HGRN_DK)
    log_f = jnp.log(lb + (1.0 - lb) * jax.nn.sigmoid(fx))
    k = (1.0 - lb) * jax.nn.sigmoid(-fx)
    v = iv.astype(jnp.float32).reshape(B, T, HGRN_HEADS, HGRN_DV)
    L = CHUNK if T % CHUNK == 0 else T
    n = T // L
    to_chunks = lambda a: jnp.moveaxis(a.reshape((B, n, L) + a.shape[2:]), 1, 0)
    S_final, o = lax.scan(hgrn_chunk_step, S0.astype(jnp.float32),
                          (to_chunks(q), to_chunks(k), to_chunks(v), to_chunks(log_f)))
    o = jnp.moveaxis(o, 0, 1).reshape(B, T, HGRN_HEADS, HGRN_DV)
    o = o * lax.rsqrt(jnp.mean(jnp.square(o), axis=-1, keepdims=True) + RMS_EPS) * norm_g
    o = o.reshape(B, T, D_MODEL) * jax.nn.silu(gate.astype(jnp.float32))
    return o.astype(x.dtype) @ w_out, S_final


def pool_mix(x, hist, pos, w_pool, scale):
    B, T, _ = x.shape
    ext = jnp.concatenate([hist.astype(x.dtype), x], axis=1)
    xf = ext.astype(jnp.float32)
    csum = jnp.concatenate([jnp.zeros((B, 1, D_MODEL), jnp.float32), jnp.cumsum(xf, axis=1)], axis=1)
    end = POOL_HIST + 1
    groups = []
    for gi, w in enumerate(POOL_WINDOWS):
        sl = slice(gi * POOL_GROUP, (gi + 1) * POOL_GROUP)
        cs = csum[:, :, sl]
        window_sum = cs[:, end:end + T] - cs[:, end - w:end - w + T]
        count = jnp.minimum(pos + 1, w).astype(jnp.float32)[None, :, None]
        groups.append(window_sum / count - xf[:, POOL_HIST:, sl])
    z = jnp.stack(groups, axis=2)
    y = jnp.einsum('btgc,gcd->btgd', z, w_pool.astype(jnp.float32)).reshape(B, T, D_MODEL) * scale
    return y.astype(x.dtype), ext[:, -POOL_HIST:]


def setup_inputs(seed: int = 0) -> dict:
    key = jax.random.key(seed)
    ks = jax.random.split(key, 20)
    nrm = lambda k, shape, s: s * jax.random.normal(k, shape, jnp.float32)
    return {
        'x_prompt': nrm(ks[0], (BATCH, SEQ, D_MODEL), 1.0),
        'x_sample': nrm(ks[1], (DEC_BATCH, DEC_SEQ, D_MODEL), 1.0),
        'cache_fox_k': nrm(ks[2], (N_FOX, DEC_BATCH, PAST_LEN, FOX_HEADS, FOX_HEAD_DIM), 1.0),
        'cache_fox_v': nrm(ks[3], (N_FOX, DEC_BATCH, PAST_LEN, FOX_HEADS, FOX_HEAD_DIM), 1.0),
        'cache_fox_lf': jax.nn.log_sigmoid(3.0 + jax.random.normal(ks[4], (N_FOX, DEC_BATCH, PAST_LEN, FOX_HEADS), jnp.float32)),
        'state_hgrn': nrm(ks[5], (N_HGRN, DEC_BATCH, HGRN_HEADS, HGRN_DK, HGRN_DV), 0.3),
        'cache_pool': nrm(ks[6], (N_POOL, DEC_BATCH, POOL_HIST, D_MODEL), 1.0),
        'w_fox_in': nrm(ks[7], (N_FOX, D_MODEL, 3 * D_MODEL + FOX_HEADS), D_MODEL ** -0.5),
        'b_fox_f': 1.0 + 5.0 * jax.random.uniform(ks[8], (N_FOX, FOX_HEADS), jnp.float32),
        'w_fox_out': nrm(ks[9], (N_FOX, D_MODEL, D_MODEL), BETA * D_MODEL ** -0.5),
        'w_hgrn_in': nrm(ks[10], (N_HGRN, D_MODEL, 4 * D_MODEL), D_MODEL ** -0.5),
        'hgrn_lb': nrm(ks[11], (DEPTH, HGRN_HEADS * HGRN_DK), 1.0),
        'hgrn_norm_g': 1.0 + nrm(ks[12], (N_HGRN, HGRN_DV), 0.1),
        'w_hgrn_out': nrm(ks[13], (N_HGRN, D_MODEL, D_MODEL), BETA * D_MODEL ** -0.5),
        'w_pool': nrm(ks[14], (N_POOL, len(POOL_WINDOWS), POOL_GROUP, POOL_GROUP), BETA * POOL_GROUP ** -0.5),
        'pool_scale': 1.0 + nrm(ks[15], (N_POOL, D_MODEL), 0.1),
        'w_ffn_in': nrm(ks[16], (DEPTH, D_MODEL, 2 * FFN_HIDDEN), D_MODEL ** -0.5),
        'w_ffn_out': nrm(ks[17], (DEPTH, FFN_HIDDEN, D_MODEL), BETA * FFN_HIDDEN ** -0.5),
        'ln_g': 1.0 + nrm(ks[18], (DEPTH, 2, D_MODEL), 0.1),
        'ln_b': nrm(ks[19], (DEPTH, 2, D_MODEL), 0.05),
    }


def reference(x_prompt, x_sample, cache_fox_k, cache_fox_v, cache_fox_lf, state_hgrn, cache_pool,
              w_fox_in, b_fox_f, w_fox_out, w_hgrn_in, hgrn_lb, hgrn_norm_g, w_hgrn_out,
              w_pool, pool_scale, w_ffn_in, w_ffn_out, ln_g, ln_b):
    lb_sched = jnp.cumsum(jax.nn.softmax(hgrn_lb.astype(jnp.float32), axis=0), axis=0)
    lb_sched = lb_sched - lb_sched[0]
    xp, xs = x_prompt, x_sample
    Bp, Tp, _ = xp.shape
    Bs, Ts, _ = xs.shape
    pos_p = jnp.arange(Tp)
    pos_s = PAST_LEN + jnp.arange(Ts)
    fk_p, fv_p, flf_p, hs_p, pl_p = [], [], [], [], []
    fk_s, fv_s, flf_s, hs_s, pl_s = [], [], [], [], []
    for i in range(DEPTH):
        j = i // N_MIXERS
        kind = i % N_MIXERS
        if kind == 0:
            mp, kp, vp, lfp = fox_prompt(xp, w_fox_in[j], b_fox_f[j], w_fox_out[j])
            ms, ks_, vs, lfs = fox_sample(xs, cache_fox_k[j], cache_fox_v[j], cache_fox_lf[j],
                                          w_fox_in[j], b_fox_f[j], w_fox_out[j])
            fk_p.append(kp); fv_p.append(vp); flf_p.append(lfp)
            fk_s.append(ks_); fv_s.append(vs); flf_s.append(lfs)
        elif kind == 1:
            S0 = jnp.zeros((Bp, HGRN_HEADS, HGRN_DK, HGRN_DV), jnp.float32)
            mp, sp = hgrn_mix(xp, S0, w_hgrn_in[j], lb_sched[i], hgrn_norm_g[j], w_hgrn_out[j])
            ms, ss = hgrn_mix(xs, state_hgrn[j], w_hgrn_in[j], lb_sched[i], hgrn_norm_g[j], w_hgrn_out[j])
            hs_p.append(sp); hs_s.append(ss)
        else:
            hist0 = jnp.zeros((Bp, POOL_HIST, D_MODEL), xp.dtype)
            mp, pp = pool_mix(xp, hist0, pos_p, w_pool[j], pool_scale[j])
            ms, ps = pool_mix(xs, cache_pool[j], pos_s, w_pool[j], pool_scale[j])
            pl_p.append(pp); pl_s.append(ps)
        xp = layer_norm(ALPHA * xp + mp, ln_g[i, 0], ln_b[i, 0])
        xs = layer_norm(ALPHA * xs + ms, ln_g[i, 0], ln_b[i, 0])
        xp = layer_norm(ALPHA * xp + swiglu_ffn(xp, w_ffn_in[i], w_ffn_out[i]), ln_g[i, 1], ln_b[i, 1])
        xs = layer_norm(ALPHA * xs + swiglu_ffn(xs, w_ffn_in[i], w_ffn_out[i]), ln_g[i, 1], ln_b[i, 1])
    return (xp, xs,
            jnp.stack(fk_p), jnp.stack(fv_p), jnp.stack(flf_p), jnp.stack(hs_p), jnp.stack(pl_p),
            jnp.stack(fk_s), jnp.stack(fv_s), jnp.stack(flf_s), jnp.stack(hs_s), jnp.stack(pl_s))
```

```python
import functools

import jax
import jax.numpy as jnp
from jax import lax
from jax.experimental import pallas as pl
from jax.experimental.pallas import tpu as pltpu

F32 = jnp.float32
BF16 = jnp.bfloat16

N_MIXERS = 3
POOL_WINDOWS = (2, 4, 8, 16)
LN_EPS = 1e-5
RMS_EPS = 1e-6

LANES = 128
VMEM_LIMIT_BYTES = 56 * 1024 * 1024
HGRN_SUB = 16
FFN_CHUNK = 256


def _params(n_axes, vmem=VMEM_LIMIT_BYTES):
    return pltpu.CompilerParams(dimension_semantics=("arbitrary",) * n_axes,
                                vmem_limit_bytes=vmem)


def _const_spec(shape):
    nd = len(shape)
    return pl.BlockSpec(shape, lambda *_: (0,) * nd, pipeline_mode=pl.Buffered(1))


def _dot(a, b):
    return jnp.dot(a, b, preferred_element_type=F32)


def _layer_norm(z, g, b):
    mu = jnp.mean(z, axis=-1, keepdims=True)
    zc = z - mu
    var = jnp.mean(zc * zc, axis=-1, keepdims=True)
    return zc * lax.rsqrt(var + LN_EPS) * g + b


def _silu(a):
    return a * jax.nn.sigmoid(a)


def _post_body(*refs, alpha, hidden, has_wo):
    if has_wo:
        x_ref, m_ref, wo_ref, g_ref, b_ref, wi_ref, w2_ref, o_ref, h_ref = refs
        mix = _dot(m_ref[...], wo_ref[...])
    else:
        x_ref, m_ref, g_ref, b_ref, wi_ref, w2_ref, o_ref, h_ref = refs
        mix = m_ref[...]
    x1 = _layer_norm(alpha * x_ref[...] + mix, g_ref[0:1, :], b_ref[0:1, :])
    xb = x1.astype(BF16)
    for c in range(hidden // FFN_CHUNK):
        lo = c * FFN_CHUNK
        a = _dot(xb, wi_ref[:, lo:lo + FFN_CHUNK])
        u = _dot(xb, wi_ref[:, hidden + lo:hidden + lo + FFN_CHUNK])
        h_ref[:, lo:lo + FFN_CHUNK] = (_silu(a) * u).astype(BF16)
    f = _dot(h_ref[...], w2_ref[...])
    o_ref[...] = _layer_norm(alpha * x1 + f, g_ref[1:2, :], b_ref[1:2, :])


def _post(x, m, wo, ln_g, ln_b, wi, w2, *, alpha, tm):
    n, d = x.shape
    hidden = w2.shape[0]
    row = pl.BlockSpec((tm, d), lambda i: (i, 0))
    has_wo = wo is not None
    args = [x, m] + ([wo] if has_wo else []) + [ln_g, ln_b, wi, w2]
    specs = [row, row] + ([_const_spec(wo.shape)] if has_wo else []) + [
        _const_spec(ln_g.shape), _const_spec(ln_b.shape), _const_spec(wi.shape), _const_spec(w2.shape)]
    return pl.pallas_call(
        functools.partial(_post_body, alpha=alpha, hidden=hidden, has_wo=has_wo),
        grid=(n // tm,),
        in_specs=specs,
        out_specs=row,
        out_shape=jax.ShapeDtypeStruct((n, d), F32),
        scratch_shapes=[pltpu.VMEM((tm, hidden), BF16)],
        compiler_params=_params(1),
        name="post_ffn",
    )(*args)


def _log_sigmoid(z):
    return jnp.minimum(z, 0.0) - jnp.log1p(jnp.exp(-jnp.abs(z)))


def _fox_proj_body(x_ref, wq_ref, wk_ref, wv_ref, wf_ref, bf_ref,
                   q_ref, k_ref, v_ref, kb_ref, vb_ref, lf_ref, *, scale, nh):
    xb = x_ref[...].astype(BF16)
    q_ref[...] = (_dot(xb, wq_ref[...]) * scale).astype(BF16)
    k = _dot(xb, wk_ref[...])
    k_ref[...] = k
    kb_ref[...] = k.astype(BF16)
    v = _dot(xb, wv_ref[...])
    v_ref[...] = v
    vb_ref[...] = v.astype(BF16)
    z = _dot(xb, wf_ref[...]) + bf_ref[...]
    lf_ref[...] = _log_sigmoid(z)[:, :nh]


def _fox_proj(x, wq, wk, wv, wf, bf, *, nh, hd, tm):
    n, d = x.shape
    row = pl.BlockSpec((tm, d), lambda i: (i, 0))
    return pl.pallas_call(
        functools.partial(_fox_proj_body, scale=hd ** -0.5, nh=nh),
        grid=(n // tm,),
        in_specs=[row, _const_spec(wq.shape), _const_spec(wk.shape), _const_spec(wv.shape),
                  _const_spec(wf.shape), _const_spec(bf.shape)],
        out_specs=[row, row, row, row, row, pl.BlockSpec((tm, nh), lambda i: (i, 0))],
        out_shape=[jax.ShapeDtypeStruct((n, d), BF16), jax.ShapeDtypeStruct((n, d), F32),
                   jax.ShapeDtypeStruct((n, d), F32), jax.ShapeDtypeStruct((n, d), BF16),
                   jax.ShapeDtypeStruct((n, d), BF16), jax.ShapeDtypeStruct((n, nh), F32)],
        compiler_params=_params(1),
        name="fox_proj",
    )(x, wq, wk, wv, wf, bf)


def _split3(x):
    hi = x.astype(BF16)
    r = x - hi.astype(F32)
    mid = r.astype(BF16)
    lo = (r - mid.astype(F32)).astype(BF16)
    return hi, mid, lo


def _cumsum_body(lf_ref, c_ref):
    nblk = lf_ref.shape[2] // LANES
    r = lax.broadcasted_iota(jnp.int32, (LANES, LANES), 0)
    c = lax.broadcasted_iota(jnp.int32, (LANES, LANES), 1)
    upper = (r <= c).astype(BF16)
    carry = jnp.zeros((lf_ref.shape[1], 1), F32)
    for j in range(nblk):
        hi, mid, lo = _split3(lf_ref[0, :, j * LANES:(j + 1) * LANES])
        cs = (_dot(hi, upper) + _dot(mid, upper) + _dot(lo, upper)) + carry
        c_ref[0, :, j * LANES:(j + 1) * LANES] = cs
        carry = cs[:, LANES - 1:LANES]


def _cumsum_time(lf_t):
    b, nh, t = lf_t.shape
    spec = pl.BlockSpec((1, nh, t), lambda i: (i, 0, 0))
    return pl.pallas_call(
        _cumsum_body, grid=(b,), in_specs=[spec], out_specs=spec,
        out_shape=jax.ShapeDtypeStruct((b, nh, t), F32),
        compiler_params=_params(1), name="fox_cumsum",
    )(lf_t)


def _attn_body(q_ref, k_ref, v_ref, c_ref, o_ref, *, blocks, hd):
    lane = lax.broadcasted_iota(jnp.int32, (1, 2 * hd), 1)
    first = lane < hd
    for (q0, q1, kd, k1) in blocks:
        tq = q1 - q0
        q2 = q_ref[0, q0:q1, :]
        k2 = k_ref[0, 0:k1, :]
        v2 = v_ref[0, 0:k1, :]
        row = lax.broadcasted_iota(jnp.int32, (tq, k1 - kd), 0)
        col = lax.broadcasted_iota(jnp.int32, (tq, k1 - kd), 1)
        outs = []
        for h in range(2):
            qh = jnp.where(first if h == 0 else jnp.logical_not(first), q2, jnp.zeros_like(q2))
            s = lax.dot_general(qh, k2, (((1,), (1,)), ((), ())), preferred_element_type=F32)
            crow = c_ref[0, 0, h:h + 1, 0:k1]
            z = s + (crow[:, kd:kd + 1] - crow)
            zd = jnp.where(col <= row, z[:, kd:], -jnp.inf)
            z = jnp.concatenate([z[:, :kd], zd], axis=1) if kd > 0 else zd
            m = jnp.max(z, axis=-1, keepdims=True)
            p = jnp.exp(z - m)
            l = jnp.sum(p, axis=-1, keepdims=True)
            outs.append(_dot(p.astype(BF16), v2) / l)
        o_ref[0, q0:q1, :] = jnp.where(first, outs[0], outs[1]).astype(BF16)


def _attention(q, k, v, c, *, blocks, hd):
    b, tq, d = q.shape
    tk = k.shape[1]
    nhp = d // (2 * hd)
    qspec = pl.BlockSpec((1, tq, 2 * hd), lambda i, j: (i, 0, j))
    kspec = pl.BlockSpec((1, tk, 2 * hd), lambda i, j: (i, 0, j))
    cspec = pl.BlockSpec((1, 1, 2, tk), lambda i, j: (i, j, 0, 0))
    return pl.pallas_call(
        functools.partial(_attn_body, blocks=blocks, hd=hd),
        grid=(b, nhp),
        in_specs=[qspec, kspec, kspec, cspec],
        out_specs=qspec,
        out_shape=jax.ShapeDtypeStruct((b, tq, d), BF16),
        compiler_params=_params(2),
        name="fox_attention",
    )(q, k, v, c)


def _hgrn_proj_body(x_ref, w_ref, lb_ref, q_ref, k_ref, b_ref, v_ref, sg_ref, *, layer):
    d = x_ref.shape[1]
    tm = x_ref.shape[0]
    xb = x_ref[...].astype(BF16)
    lbp = lb_ref[...]
    e = jnp.exp(lbp - jnp.max(lbp, axis=0, keepdims=True))
    lb = jnp.sum(e[1:layer + 1, :], axis=0, keepdims=True) / jnp.sum(e, axis=0, keepdims=True)
    q_ref[...] = _silu(_dot(xb, w_ref[:, 0:d]))
    fx = _dot(xb, w_ref[:, d:2 * d])
    g = jnp.log(lb + (1.0 - lb) * jax.nn.sigmoid(fx))
    k_ref[...] = (1.0 - lb) * jax.nn.sigmoid(-fx)
    pos = lax.broadcasted_iota(jnp.int32, (tm, 1), 0) % HGRN_SUB
    sh = 1
    while sh < HGRN_SUB:
        g = g + jnp.where(pos >= sh, pltpu.roll(g, sh, axis=0), 0.0)
        sh *= 2
    b_ref[...] = g
    v_ref[...] = _dot(xb, w_ref[:, 2 * d:3 * d])
    sg_ref[...] = _silu(_dot(xb, w_ref[:, 3 * d:4 * d]))


def _hgrn_proj(x, w, lb, *, layer, tm):
    n, d = x.shape
    row = pl.BlockSpec((tm, d), lambda i: (i, 0))
    out = jax.ShapeDtypeStruct((n, d), F32)
    return pl.pallas_call(
        functools.partial(_hgrn_proj_body, layer=layer),
        grid=(n // tm,),
        in_specs=[row, _const_spec(w.shape), _const_spec(lb.shape)],
        out_specs=[row] * 5, out_shape=[out] * 5,
        compiler_params=_params(1), name="hgrn_proj",
    )(x, w, lb)


def _hgrn_rec_body(q_ref, k_ref, b_ref, v_ref, sg_ref, ng_ref, s0_ref, o_ref, sT_ref, state,
                   *, nh, dk):
    sub = HGRN_SUB
    tt = q_ref.shape[1]
    ti = pl.program_id(1)

    @pl.when(ti == 0)
    def _():
        state[...] = s0_ref[0]

    row = lax.broadcasted_iota(jnp.int32, (sub, 1), 0)
    ng = ng_ref[...]

    def step(i, carry):
        r0 = pl.multiple_of(i * sub, sub)
        for h in range(nh):
            ls = slice(h * dk, (h + 1) * dk)
            qs = q_ref[0, pl.ds(r0, sub), ls]
            ks = k_ref[0, pl.ds(r0, sub), ls]
            bs = b_ref[0, pl.ds(r0, sub), ls]
            vs = v_ref[0, pl.ds(r0, sub), ls]
            bl = bs[sub - 1:sub, :]
            od = jnp.zeros((sub, dk), F32)
            for s in range(sub):
                e = jnp.exp(jnp.where(row >= s, bs - bs[s:s + 1, :], -jnp.inf))
                r = jnp.sum(qs * ks[s:s + 1, :] * e, axis=-1, keepdims=True)
                od = od + r * vs[s:s + 1, :]
            st = state[h]
            qt = (qs * jnp.exp(bs)).astype(BF16)
            o = od + lax.dot_general(qt, st.astype(BF16), (((1,), (1,)), ((), ())),
                                     preferred_element_type=F32)
            kt = (ks * jnp.exp(bl - bs)).astype(BF16)
            upd = lax.dot_general(vs.astype(BF16), kt, (((0,), (0,)), ((), ())),
                                  preferred_element_type=F32)
            state[h] = jnp.exp(bl) * st + upd
            ms = jnp.mean(o * o, axis=-1, keepdims=True)
            on = o * lax.rsqrt(ms + RMS_EPS) * ng * sg_ref[0, pl.ds(r0, sub), ls]
            o_ref[0, pl.ds(r0, sub), ls] = on.astype(BF16)
        return carry

    lax.fori_loop(0, tt // sub, step, 0)

    @pl.when(ti == pl.num_programs(1) - 1)
    def _():
        sT_ref[0] = state[...]


def _hgrn_rec(q, k, b, v, sg, ng, s0_t, *, tt):
    bsz, t, d = q.shape
    nh, dv, dk = s0_t.shape[1:]
    blk = pl.BlockSpec((1, tt, d), lambda i, j: (i, j, 0))
    sspec = pl.BlockSpec((1, nh, dv, dk), lambda i, j: (i, 0, 0, 0))
    return pl.pallas_call(
        functools.partial(_hgrn_rec_body, nh=nh, dk=dk),
        grid=(bsz, t // tt),
        in_specs=[blk] * 5 + [pl.BlockSpec(ng.shape, lambda i, j: (0, 0)), sspec],
        out_specs=[blk, sspec],
        out_shape=[jax.ShapeDtypeStruct((bsz, t, d), BF16),
                   jax.ShapeDtypeStruct((bsz, nh, dv, dk), F32)],
        scratch_shapes=[pltpu.VMEM((nh, dv, dk), F32)],
        compiler_params=_params(2), name="hgrn_recurrence",
    )(q, k, b, v, sg, ng, s0_t)


def _pool_body(x_ref, halo_ref, hist_ref, w_ref, sc_ref, y_ref, ext, *, pos0, hpad):
    tm = x_ref.shape[1]
    d = x_ref.shape[2]
    ti = pl.program_id(1)
    gw = d // len(POOL_WINDOWS)

    @pl.when(ti == 0)
    def _():
        ext[0:hpad, :] = hist_ref[0]

    @pl.when(ti > 0)
    def _():
        ext[0:hpad, :] = halo_ref[0]

    ext[hpad:hpad + tm, :] = x_ref[0]
    pos = pos0 + ti * tm + lax.broadcasted_iota(jnp.int32, (tm, 1), 0)
    ys = []
    for gi, w in enumerate(POOL_WINDOWS):
        ls = slice(gi * gw, (gi + 1) * gw)
        xg = ext[hpad:hpad + tm, ls]
        ws = xg
        for j in range(1, w):
            ws = ws + ext[hpad - j:hpad - j + tm, ls]
        count = jnp.minimum(pos + 1, w).astype(F32)
        z = ws / count - xg
        ys.append(_dot(z.astype(BF16), w_ref[gi]))
    y_ref[0] = jnp.concatenate(ys, axis=1) * sc_ref[...]


def _pool(x, hist, w, scale, *, pos0, tm):
    bsz, t, d = x.shape
    hpad = hist.shape[1]
    blk = pl.BlockSpec((1, tm, d), lambda i, j: (i, j, 0))
    per = tm // hpad
    halo = pl.BlockSpec((1, hpad, d), lambda i, j: (i, jnp.maximum(j * per - 1, 0), 0))
    return pl.pallas_call(
        functools.partial(_pool_body, pos0=pos0, hpad=hpad),
        grid=(bsz, t // tm),
        in_specs=[blk, halo, pl.BlockSpec((1, hpad, d), lambda i, j: (i, 0, 0)),
                  pl.BlockSpec(w.shape, lambda i, j: (0, 0, 0)),
                  pl.BlockSpec(scale.shape, lambda i, j: (0, 0))],
        out_specs=blk,
        out_shape=jax.ShapeDtypeStruct((bsz, t, d), F32),
        scratch_shapes=[pltpu.VMEM((hpad + tm, d), F32)],
        compiler_params=_params(2), name="pool_mix",
    )(x, x, hist, w, scale)


def _causal_blocks(t, tq):
    return tuple((lo, lo + tq, lo, lo + tq) for lo in range(0, t, tq))


def kernel(x_prompt, x_sample, cache_fox_k, cache_fox_v, cache_fox_lf, state_hgrn, cache_pool,
           w_fox_in, b_fox_f, w_fox_out, w_hgrn_in, hgrn_lb, hgrn_norm_g, w_hgrn_out,
           w_pool, pool_scale, w_ffn_in, w_ffn_out, ln_g, ln_b):
    bp, tp, d = x_prompt.shape
    bs, ts, _ = x_sample.shape
    depth = w_ffn_in.shape[0]
    nh = b_fox_f.shape[1]
    hd = d // nh
    past = cache_fox_k.shape[2]
    hist = cache_pool.shape[2]
    hg_heads, hg_dk, hg_dv = state_hgrn.shape[2:]
    alpha = (2 * depth) ** 0.25
    np_, ns = bp * tp, bs * ts
    tm_p = 512
    tq = 256
    hpad = 16

    xp = x_prompt.reshape(np_, d)
    xs = x_sample.reshape(ns, d)
    outs = {name: [] for name in ("fk_p", "fv_p", "flf_p", "hs_p", "pl_p",
                                  "fk_s", "fv_s", "flf_s", "hs_s", "pl_s")}

    for i in range(depth):
        j = i // N_MIXERS
        kind = i % N_MIXERS
        wi = w_ffn_in[i].astype(BF16)
        w2 = w_ffn_out[i].astype(BF16)
        wo = None
        if kind == 0:
            w_in = w_fox_in[j]
            wq = w_in[:, :d].astype(BF16)
            wk = w_in[:, d:2 * d].astype(BF16)
            wv = w_in[:, 2 * d:3 * d].astype(BF16)
            wf = jnp.pad(w_in[:, 3 * d:], ((0, 0), (0, LANES - nh))).astype(BF16)
            bf = jnp.pad(b_fox_f[j], (0, LANES - nh)).reshape(1, LANES)
            wo = w_fox_out[j].astype(BF16)

            q, k, v, kb, vb, lf = _fox_proj(xp, wq, wk, wv, wf, bf, nh=nh, hd=hd, tm=tm_p)
            c = _cumsum_time(jnp.swapaxes(lf.reshape(bp, tp, nh), 1, 2))
            mp = _attention(q.reshape(bp, tp, d), kb.reshape(bp, tp, d), vb.reshape(bp, tp, d),
                            c.reshape(bp, nh // 2, 2, tp), blocks=_causal_blocks(tp, tq), hd=hd)
            mp = mp.reshape(np_, d)
            outs["fk_p"].append(k.reshape(bp, tp, nh, hd))
            outs["fv_p"].append(v.reshape(bp, tp, nh, hd))
            outs["flf_p"].append(lf.reshape(bp, tp, nh))

            q, k, v, kb, vb, lf = _fox_proj(xs, wq, wk, wv, wf, bf, nh=nh, hd=hd, tm=ns)
            tk = past + ts
            tk_pad = -(-tk // LANES) * LANES
            padt = ((0, 0), (0, tk_pad - tk), (0, 0))
            k_all = jnp.pad(jnp.concatenate(
                [cache_fox_k[j].reshape(bs, past, d).astype(BF16), kb.reshape(bs, ts, d)], axis=1), padt)
            v_all = jnp.pad(jnp.concatenate(
                [cache_fox_v[j].reshape(bs, past, d).astype(BF16), vb.reshape(bs, ts, d)], axis=1), padt)
            lf_all = jnp.pad(jnp.concatenate([cache_fox_lf[j], lf.reshape(bs, ts, nh)], axis=1), padt)
            c = _cumsum_time(jnp.swapaxes(lf_all, 1, 2))
            ms = _attention(q.reshape(bs, ts, d), k_all, v_all, c.reshape(bs, nh // 2, 2, tk_pad),
                            blocks=((0, ts, past, tk_pad),), hd=hd)
            ms = ms.reshape(ns, d)
            outs["fk_s"].append(k.reshape(bs, ts, nh, hd))
            outs["fv_s"].append(v.reshape(bs, ts, nh, hd))
            outs["flf_s"].append(lf.reshape(bs, ts, nh))
        elif kind == 1:
            w_in = w_hgrn_in[j].astype(BF16)
            wo = w_hgrn_out[j].astype(BF16)
            ng = hgrn_norm_g[j].reshape(1, hg_dv)
            s0 = jnp.zeros((bp, hg_heads, hg_dv, hg_dk), F32)
            parts = _hgrn_proj(xp, w_in, hgrn_lb, layer=i, tm=tm_p)
            mp, st = _hgrn_rec(*[a.reshape(bp, tp, d) for a in parts], ng, s0, tt=256)
            mp = mp.reshape(np_, d)
            outs["hs_p"].append(jnp.swapaxes(st, 2, 3))
            parts = _hgrn_proj(xs, w_in, hgrn_lb, layer=i, tm=ns)
            ms, st = _hgrn_rec(*[a.reshape(bs, ts, d) for a in parts], ng,
                               jnp.swapaxes(state_hgrn[j], 2, 3), tt=ts)
            ms = ms.reshape(ns, d)
            outs["hs_s"].append(jnp.swapaxes(st, 2, 3))
        else:
            wp = w_pool[j].astype(BF16)
            sc = pool_scale[j].reshape(1, d)
            xp3 = xp.reshape(bp, tp, d)
            xs3 = xs.reshape(bs, ts, d)
            hist_p = jnp.zeros((bp, hpad, d), F32)
            hist_s = jnp.pad(cache_pool[j], ((0, 0), (hpad - hist, 0), (0, 0)))
            mp = _pool(xp3, hist_p, wp, sc, pos0=0, tm=tm_p).reshape(np_, d)
            ms = _pool(xs3, hist_s, wp, sc, pos0=past, tm=ts).reshape(ns, d)
            outs["pl_p"].append(xp3[:, tp - hist:])
            outs["pl_s"].append(jnp.concatenate([cache_pool[j], xs3], axis=1)[:, ts:])
        xp = _post(xp, mp, wo, ln_g[i], ln_b[i], wi, w2, alpha=alpha, tm=tm_p)
        xs = _post(xs, ms, wo, ln_g[i], ln_b[i], wi, w2, alpha=alpha, tm=ns)

    stack = lambda name: jnp.stack(outs[name])
    return (xp.reshape(bp, tp, d), xs.reshape(bs, ts, d),
            stack("fk_p"), stack("fv_p"), stack("flf_p"), stack("hs_p"), stack("pl_p"),
            stack("fk_s"), stack("fv_s"), stack("flf_s"), stack("hs_s"), stack("pl_s"))
```

```python
import functools

import jax
import jax.numpy as jnp
from jax import lax
from jax.experimental import pallas as pl
from jax.experimental.pallas import tpu as pltpu

F32 = jnp.float32
BF16 = jnp.bfloat16

N_MIXERS = 3
POOL_WINDOWS = (2, 4, 8, 16)
LN_EPS = 1e-5
RMS_EPS = 1e-6

LANES = 128
VMEM_LIMIT_BYTES = 56 * 1024 * 1024
HGRN_CHUNK = 64
HGRN_SUB = 16
HGRN_SAFE_EXPONENT = 80.0
FFN_CHUNK = 256


def _params(n_axes, vmem=VMEM_LIMIT_BYTES):
    return pltpu.CompilerParams(dimension_semantics=("arbitrary",) * n_axes,
                                vmem_limit_bytes=vmem)


def _const_spec(shape):
    nd = len(shape)
    return pl.BlockSpec(shape, lambda *_: (0,) * nd, pipeline_mode=pl.Buffered(1))


def _dot(a, b):
    return jnp.dot(a, b, preferred_element_type=F32)


def _layer_norm(z, g, b):
    mu = jnp.mean(z, axis=-1, keepdims=True)
    zc = z - mu
    var = jnp.mean(zc * zc, axis=-1, keepdims=True)
    return zc * lax.rsqrt(var + LN_EPS) * g + b


def _silu(a):
    return a * jax.nn.sigmoid(a)


def _post_body(*refs, alpha, hidden, has_wo):
    if has_wo:
        x_ref, m_ref, wo_ref, g_ref, b_ref, wi_ref, w2_ref, o_ref, h_ref = refs
        mix = _dot(m_ref[...], wo_ref[...])
    else:
        x_ref, m_ref, g_ref, b_ref, wi_ref, w2_ref, o_ref, h_ref = refs
        mix = m_ref[...]
    x1 = _layer_norm(alpha * x_ref[...] + mix, g_ref[0:1, :], b_ref[0:1, :])
    xb = x1.astype(BF16)
    for c in range(hidden // FFN_CHUNK):
        lo = c * FFN_CHUNK
        a = _dot(xb, wi_ref[:, lo:lo + FFN_CHUNK])
        u = _dot(xb, wi_ref[:, hidden + lo:hidden + lo + FFN_CHUNK])
        h_ref[:, lo:lo + FFN_CHUNK] = (_silu(a) * u).astype(BF16)
    f = _dot(h_ref[...], w2_ref[...])
    o_ref[...] = _layer_norm(alpha * x1 + f, g_ref[1:2, :], b_ref[1:2, :])


def _post(x, m, wo, ln_g, ln_b, wi, w2, *, alpha, tm):
    n, d = x.shape
    hidden = w2.shape[0]
    row = pl.BlockSpec((tm, d), lambda i: (i, 0))
    has_wo = wo is not None
    args = [x, m] + ([wo] if has_wo else []) + [ln_g, ln_b, wi, w2]
    specs = [row, row] + ([_const_spec(wo.shape)] if has_wo else []) + [
        _const_spec(ln_g.shape), _const_spec(ln_b.shape), _const_spec(wi.shape), _const_spec(w2.shape)]
    return pl.pallas_call(
        functools.partial(_post_body, alpha=alpha, hidden=hidden, has_wo=has_wo),
        grid=(n // tm,),
        in_specs=specs,
        out_specs=row,
        out_shape=jax.ShapeDtypeStruct((n, d), F32),
        scratch_shapes=[pltpu.VMEM((tm, hidden), BF16)],
        compiler_params=_params(1),
        name="post_ffn",
    )(*args)


def _log_sigmoid(z):
    return jnp.minimum(z, 0.0) - jnp.log1p(jnp.exp(-jnp.abs(z)))


def _fox_proj_body(x_ref, wq_ref, wk_ref, wv_ref, wf_ref, bf_ref,
                   q_ref, k_ref, v_ref, kb_ref, vb_ref, lf_ref, *, scale, nh):
    xb = x_ref[...].astype(BF16)
    q_ref[...] = (_dot(xb, wq_ref[...]) * scale).astype(BF16)
    k = _dot(xb, wk_ref[...])
    k_ref[...] = k
    kb_ref[...] = k.astype(BF16)
    v = _dot(xb, wv_ref[...])
    v_ref[...] = v
    vb_ref[...] = v.astype(BF16)
    z = _dot(xb, wf_ref[...]) + bf_ref[...]
    lf_ref[...] = _log_sigmoid(z)[:, :nh]


def _fox_proj(x, wq, wk, wv, wf, bf, *, nh, hd, tm):
    n, d = x.shape
    row = pl.BlockSpec((tm, d), lambda i: (i, 0))
    return pl.pallas_call(
        functools.partial(_fox_proj_body, scale=hd ** -0.5, nh=nh),
        grid=(n // tm,),
        in_specs=[row, _const_spec(wq.shape), _const_spec(wk.shape), _const_spec(wv.shape),
                  _const_spec(wf.shape), _const_spec(bf.shape)],
        out_specs=[row, row, row, row, row, pl.BlockSpec((tm, nh), lambda i: (i, 0))],
        out_shape=[jax.ShapeDtypeStruct((n, d), BF16), jax.ShapeDtypeStruct((n, d), F32),
                   jax.ShapeDtypeStruct((n, d), F32), jax.ShapeDtypeStruct((n, d), BF16),
                   jax.ShapeDtypeStruct((n, d), BF16), jax.ShapeDtypeStruct((n, nh), F32)],
        compiler_params=_params(1),
        name="fox_proj",
    )(x, wq, wk, wv, wf, bf)


def _split3(x):
    hi = x.astype(BF16)
    r = x - hi.astype(F32)
    mid = r.astype(BF16)
    lo = (r - mid.astype(F32)).astype(BF16)
    return hi, mid, lo


def _cumsum_body(lf_ref, c_ref):
    nblk = lf_ref.shape[2] // LANES
    r = lax.broadcasted_iota(jnp.int32, (LANES, LANES), 0)
    c = lax.broadcasted_iota(jnp.int32, (LANES, LANES), 1)
    upper = (r <= c).astype(BF16)
    carry = jnp.zeros((lf_ref.shape[1], 1), F32)
    for j in range(nblk):
        hi, mid, lo = _split3(lf_ref[0, :, j * LANES:(j + 1) * LANES])
        cs = (_dot(hi, upper) + _dot(mid, upper) + _dot(lo, upper)) + carry
        c_ref[0, :, j * LANES:(j + 1) * LANES] = cs
        carry = cs[:, LANES - 1:LANES]


def _cumsum_time(lf_t):
    b, nh, t = lf_t.shape
    spec = pl.BlockSpec((1, nh, t), lambda i: (i, 0, 0))
    return pl.pallas_call(
        _cumsum_body, grid=(b,), in_specs=[spec], out_specs=spec,
        out_shape=jax.ShapeDtypeStruct((b, nh, t), F32),
        compiler_params=_params(1), name="fox_cumsum",
    )(lf_t)


def _attn_body(q_ref, k_ref, v_ref, c_ref, o_ref, *, blocks, hd):
    lane = lax.broadcasted_iota(jnp.int32, (1, 2 * hd), 1)
    first = lane < hd
    for (q0, q1, kd, k1) in blocks:
        tq = q1 - q0
        q2 = q_ref[0, q0:q1, :]
        k2 = k_ref[0, 0:k1, :]
        v2 = v_ref[0, 0:k1, :]
        row = lax.broadcasted_iota(jnp.int32, (tq, k1 - kd), 0)
        col = lax.broadcasted_iota(jnp.int32, (tq, k1 - kd), 1)
        outs = []
        for h in range(2):
            qh = jnp.where(first if h == 0 else jnp.logical_not(first), q2, jnp.zeros_like(q2))
            s = lax.dot_general(qh, k2, (((1,), (1,)), ((), ())), preferred_element_type=F32)
            crow = c_ref[0, 0, h:h + 1, 0:k1]
            z = s + (crow[:, kd:kd + 1] - crow)
            zd = jnp.where(col <= row, z[:, kd:], -jnp.inf)
            z = jnp.concatenate([z[:, :kd], zd], axis=1) if kd > 0 else zd
            m = jnp.max(z, axis=-1, keepdims=True)
            p = jnp.exp(z - m)
            l = jnp.sum(p, axis=-1, keepdims=True)
            outs.append(_dot(p.astype(BF16), v2) / l)
        o_ref[0, q0:q1, :] = jnp.where(first, outs[0], outs[1]).astype(BF16)


def _attention(q, k, v, c, *, blocks, hd):
    b, tq, d = q.shape
    tk = k.shape[1]
    nhp = d // (2 * hd)
    qspec = pl.BlockSpec((1, tq, 2 * hd), lambda i, j: (i, 0, j))
    kspec = pl.BlockSpec((1, tk, 2 * hd), lambda i, j: (i, 0, j))
    cspec = pl.BlockSpec((1, 1, 2, tk), lambda i, j: (i, j, 0, 0))
    return pl.pallas_call(
        functools.partial(_attn_body, blocks=blocks, hd=hd),
        grid=(b, nhp),
        in_specs=[qspec, kspec, kspec, cspec],
        out_specs=qspec,
        out_shape=jax.ShapeDtypeStruct((b, tq, d), BF16),
        compiler_params=_params(2),
        name="fox_attention",
    )(q, k, v, c)


def _nt(a, b):
    return lax.dot_general(a, b, (((1,), (1,)), ((), ())), preferred_element_type=F32)


def _tn(a, b):
    return lax.dot_general(a, b, (((0,), (0,)), ((), ())), preferred_element_type=F32)


def _hgrn_body(x_ref, w_ref, lb_ref, ng_ref, s0_ref, o_ref, sT_ref,
               state, q_s, k_s, b_s, v_s, g_s, *, layer, nh, dk, chunk):
    tt, d = x_ref.shape[1], x_ref.shape[2]
    ti = pl.program_id(1)

    @pl.when(ti == 0)
    def _():
        state[...] = s0_ref[0]

    xb = x_ref[0].astype(BF16)
    lbp = lb_ref[...]
    e = jnp.exp(lbp - jnp.max(lbp, axis=0, keepdims=True))
    lb = jnp.sum(e[1:layer + 1, :], axis=0, keepdims=True) / jnp.sum(e, axis=0, keepdims=True)
    q_s[...] = _silu(_dot(xb, w_ref[:, 0:d]))
    fx = _dot(xb, w_ref[:, d:2 * d])
    k_s[...] = (1.0 - lb) * jax.nn.sigmoid(-fx)
    b = jnp.log(lb + (1.0 - lb) * jax.nn.sigmoid(fx))
    pos = lax.broadcasted_iota(jnp.int32, (tt, 1), 0) % chunk
    sh = 1
    while sh < chunk:
        b = b + jnp.where(pos >= sh, pltpu.roll(b, sh, axis=0), 0.0)
        sh *= 2
    b_s[...] = b
    v_s[...] = _dot(xb, w_ref[:, 2 * d:3 * d])
    g_s[...] = _silu(_dot(xb, w_ref[:, 3 * d:4 * d]))
    factorable = jnp.min(b) >= -HGRN_SAFE_EXPONENT
    ng = ng_ref[...]

    def finish(h, r0, rows, o, st, bl, upd):
        ls = slice(h * dk, (h + 1) * dk)
        state[h] = jnp.exp(bl) * st + upd
        ms = jnp.mean(o * o, axis=-1, keepdims=True)
        on = o * lax.rsqrt(ms + RMS_EPS) * ng * g_s[pl.ds(r0, rows), ls]
        o_ref[0, pl.ds(r0, rows), ls] = on.astype(BF16)

    @pl.when(factorable)
    def _():
        causal = (lax.broadcasted_iota(jnp.int32, (chunk, chunk), 0)
                  >= lax.broadcasted_iota(jnp.int32, (chunk, chunk), 1))

        def step(i, carry):
            r0 = pl.multiple_of(i * chunk, chunk)
            for h in range(nh):
                ls = slice(h * dk, (h + 1) * dk)
                bs = b_s[pl.ds(r0, chunk), ls]
                bl = bs[chunk - 1:chunk, :]
                vb = v_s[pl.ds(r0, chunk), ls].astype(BF16)
                qt = (q_s[pl.ds(r0, chunk), ls] * jnp.exp(bs)).astype(BF16)
                kh = k_s[pl.ds(r0, chunk), ls] * jnp.exp(-bs)
                kt = (kh * jnp.exp(bl)).astype(BF16)
                sc = jnp.where(causal, _nt(qt, kh.astype(BF16)), 0.0).astype(BF16)
                st = state[h]
                o = _dot(sc, vb) + _nt(qt, st.astype(BF16))
                finish(h, r0, chunk, o, st, bl, _tn(vb, kt))
            return carry

        lax.fori_loop(0, tt // chunk, step, 0)

    @pl.when(jnp.logical_not(factorable))
    def _():
        sub = min(HGRN_SUB, chunk)
        row = lax.broadcasted_iota(jnp.int32, (sub, 1), 0)

        def step(i, carry):
            r0 = pl.multiple_of(i * sub, sub)
            inside = (r0 % chunk) != 0
            prev = pl.multiple_of(jnp.maximum(r0 - 8, 0), 8)
            for h in range(nh):
                ls = slice(h * dk, (h + 1) * dk)
                qs = q_s[pl.ds(r0, sub), ls]
                ks = k_s[pl.ds(r0, sub), ls]
                vs = v_s[pl.ds(r0, sub), ls]
                base = jnp.where(inside, b_s[pl.ds(prev, 8), ls][7:8, :], 0.0)
                bs = b_s[pl.ds(r0, sub), ls] - base
                bl = bs[sub - 1:sub, :]
                od = jnp.zeros((sub, dk), F32)
                for s in range(sub):
                    dec = jnp.exp(jnp.where(row >= s, bs - bs[s:s + 1, :], -jnp.inf))
                    r = jnp.sum(qs * ks[s:s + 1, :] * dec, axis=-1, keepdims=True)
                    od = od + r * vs[s:s + 1, :]
                st = state[h]
                qt = (qs * jnp.exp(bs)).astype(BF16)
                kt = (ks * jnp.exp(bl - bs)).astype(BF16)
                o = od + _nt(qt, st.astype(BF16))
                finish(h, r0, sub, o, st, bl, _tn(vs.astype(BF16), kt))
            return carry

        lax.fori_loop(0, tt // sub, step, 0)

    @pl.when(ti == pl.num_programs(1) - 1)
    def _():
        sT_ref[0] = state[...]


def _hgrn(x, w, lb, ng, s0_t, *, layer, tt):
    bsz, t, d = x.shape
    nh, dv, dk = s0_t.shape[1:]
    chunk = min(HGRN_CHUNK, tt)
    blk = pl.BlockSpec((1, tt, d), lambda i, j: (i, j, 0))
    sspec = pl.BlockSpec((1, nh, dv, dk), lambda i, j: (i, 0, 0, 0))
    return pl.pallas_call(
        functools.partial(_hgrn_body, layer=layer, nh=nh, dk=dk, chunk=chunk),
        grid=(bsz, t // tt),
        in_specs=[blk, _const_spec(w.shape), _const_spec(lb.shape), _const_spec(ng.shape), sspec],
        out_specs=[blk, sspec],
        out_shape=[jax.ShapeDtypeStruct((bsz, t, d), BF16),
                   jax.ShapeDtypeStruct((bsz, nh, dv, dk), F32)],
        scratch_shapes=[pltpu.VMEM((nh, dv, dk), F32)] + [pltpu.VMEM((tt, d), F32)] * 5,
        compiler_params=_params(2), name="hgrn_mix",
    )(x, w, lb, ng, s0_t)


def _pool_body(x_ref, halo_ref, hist_ref, w_ref, sc_ref, y_ref, ext, *, pos0, hpad):
    tm = x_ref.shape[1]
    d = x_ref.shape[2]
    ti = pl.program_id(1)
    gw = d // len(POOL_WINDOWS)

    @pl.when(ti == 0)
    def _():
        ext[0:hpad, :] = hist_ref[0]

    @pl.when(ti > 0)
    def _():
        ext[0:hpad, :] = halo_ref[0]

    ext[hpad:hpad + tm, :] = x_ref[0]
    pos = pos0 + ti * tm + lax.broadcasted_iota(jnp.int32, (tm, 1), 0)
    ys = []
    for gi, w in enumerate(POOL_WINDOWS):
        ls = slice(gi * gw, (gi + 1) * gw)
        xg = ext[hpad:hpad + tm, ls]
        ws = xg
        for j in range(1, w):
            ws = ws + ext[hpad - j:hpad - j + tm, ls]
        count = jnp.minimum(pos + 1, w).astype(F32)
        z = ws / count - xg
        ys.append(_dot(z.astype(BF16), w_ref[gi]))
    y_ref[0] = jnp.concatenate(ys, axis=1) * sc_ref[...]


def _pool(x, hist, w, scale, *, pos0, tm):
    bsz, t, d = x.shape
    hpad = hist.shape[1]
    blk = pl.BlockSpec((1, tm, d), lambda i, j: (i, j, 0))
    per = tm // hpad
    halo = pl.BlockSpec((1, hpad, d), lambda i, j: (i, jnp.maximum(j * per - 1, 0), 0))
    return pl.pallas_call(
        functools.partial(_pool_body, pos0=pos0, hpad=hpad),
        grid=(bsz, t // tm),
        in_specs=[blk, halo, pl.BlockSpec((1, hpad, d), lambda i, j: (i, 0, 0)),
                  pl.BlockSpec(w.shape, lambda i, j: (0, 0, 0)),
                  pl.BlockSpec(scale.shape, lambda i, j: (0, 0))],
        out_specs=blk,
        out_shape=jax.ShapeDtypeStruct((bsz, t, d), F32),
        scratch_shapes=[pltpu.VMEM((hpad + tm, d), F32)],
        compiler_params=_params(2), name="pool_mix",
    )(x, x, hist, w, scale)


def _causal_blocks(t, tq):
    return tuple((lo, lo + tq, lo, lo + tq) for lo in range(0, t, tq))


def kernel(x_prompt, x_sample, cache_fox_k, cache_fox_v, cache_fox_lf, state_hgrn, cache_pool,
           w_fox_in, b_fox_f, w_fox_out, w_hgrn_in, hgrn_lb, hgrn_norm_g, w_hgrn_out,
           w_pool, pool_scale, w_ffn_in, w_ffn_out, ln_g, ln_b):
    bp, tp, d = x_prompt.shape
    bs, ts, _ = x_sample.shape
    depth = w_ffn_in.shape[0]
    nh = b_fox_f.shape[1]
    hd = d // nh
    past = cache_fox_k.shape[2]
    hist = cache_pool.shape[2]
    hg_heads, hg_dk, hg_dv = state_hgrn.shape[2:]
    alpha = (2 * depth) ** 0.25
    np_, ns = bp * tp, bs * ts
    tm_p = 512
    tq = 256
    hpad = 16

    xp = x_prompt.reshape(np_, d)
    xs = x_sample.reshape(ns, d)
    outs = {name: [] for name in ("fk_p", "fv_p", "flf_p", "hs_p", "pl_p",
                                  "fk_s", "fv_s", "flf_s", "hs_s", "pl_s")}

    for i in range(depth):
        j = i // N_MIXERS
        kind = i % N_MIXERS
        wi = w_ffn_in[i].astype(BF16)
        w2 = w_ffn_out[i].astype(BF16)
        wo = None
        if kind == 0:
            w_in = w_fox_in[j]
            wq = w_in[:, :d].astype(BF16)
            wk = w_in[:, d:2 * d].astype(BF16)
            wv = w_in[:, 2 * d:3 * d].astype(BF16)
            wf = jnp.pad(w_in[:, 3 * d:], ((0, 0), (0, LANES - nh))).astype(BF16)
            bf = jnp.pad(b_fox_f[j], (0, LANES - nh)).reshape(1, LANES)
            wo = w_fox_out[j].astype(BF16)

            q, k, v, kb, vb, lf = _fox_proj(xp, wq, wk, wv, wf, bf, nh=nh, hd=hd, tm=tm_p)
            c = _cumsum_time(jnp.swapaxes(lf.reshape(bp, tp, nh), 1, 2))
            mp = _attention(q.reshape(bp, tp, d), kb.reshape(bp, tp, d), vb.reshape(bp, tp, d),
                            c.reshape(bp, nh // 2, 2, tp), blocks=_causal_blocks(tp, tq), hd=hd)
            mp = mp.reshape(np_, d)
            outs["fk_p"].append(k.reshape(bp, tp, nh, hd))
            outs["fv_p"].append(v.reshape(bp, tp, nh, hd))
            outs["flf_p"].append(lf.reshape(bp, tp, nh))

            q, k, v, kb, vb, lf = _fox_proj(xs, wq, wk, wv, wf, bf, nh=nh, hd=hd, tm=ns)
            tk = past + ts
            tk_pad = -(-tk // LANES) * LANES
            padt = ((0, 0), (0, tk_pad - tk), (0, 0))
            k_all = jnp.pad(jnp.concatenate(
                [cache_fox_k[j].reshape(bs, past, d).astype(BF16), kb.reshape(bs, ts, d)], axis=1), padt)
            v_all = jnp.pad(jnp.concatenate(
                [cache_fox_v[j].reshape(bs, past, d).astype(BF16), vb.reshape(bs, ts, d)], axis=1), padt)
            lf_all = jnp.pad(jnp.concatenate([cache_fox_lf[j], lf.reshape(bs, ts, nh)], axis=1), padt)
            c = _cumsum_time(jnp.swapaxes(lf_all, 1, 2))
            ms = _attention(q.reshape(bs, ts, d), k_all, v_all, c.reshape(bs, nh // 2, 2, tk_pad),
                            blocks=((0, ts, past, tk_pad),), hd=hd)
            ms = ms.reshape(ns, d)
            outs["fk_s"].append(k.reshape(bs, ts, nh, hd))
            outs["fv_s"].append(v.reshape(bs, ts, nh, hd))
            outs["flf_s"].append(lf.reshape(bs, ts, nh))
        elif kind == 1:
            w_in = w_hgrn_in[j].astype(BF16)
            wo = w_hgrn_out[j].astype(BF16)
            ng = hgrn_norm_g[j].reshape(1, hg_dv)
            s0 = jnp.zeros((bp, hg_heads, hg_dv, hg_dk), F32)
            mp, st = _hgrn(xp.reshape(bp, tp, d), w_in, hgrn_lb, ng, s0, layer=i, tt=tm_p)
            mp = mp.reshape(np_, d)
            outs["hs_p"].append(jnp.swapaxes(st, 2, 3))
            ms, st = _hgrn(xs.reshape(bs, ts, d), w_in, hgrn_lb, ng,
                           jnp.swapaxes(state_hgrn[j], 2, 3), layer=i, tt=ts)
            ms = ms.reshape(ns, d)
            outs["hs_s"].append(jnp.swapaxes(st, 2, 3))
        else:
            wp = w_pool[j].astype(BF16)
            sc = pool_scale[j].reshape(1, d)
            xp3 = xp.reshape(bp, tp, d)
            xs3 = xs.reshape(bs, ts, d)
            hist_p = jnp.zeros((bp, hpad, d), F32)
            hist_s = jnp.pad(cache_pool[j], ((0, 0), (hpad - hist, 0), (0, 0)))
            mp = _pool(xp3, hist_p, wp, sc, pos0=0, tm=tm_p).reshape(np_, d)
            ms = _pool(xs3, hist_s, wp, sc, pos0=past, tm=ts).reshape(ns, d)
            outs["pl_p"].append(xp3[:, tp - hist:])
            outs["pl_s"].append(jnp.concatenate([cache_pool[j], xs3], axis=1)[:, ts:])
        xp = _post(xp, mp, wo, ln_g[i], ln_b[i], wi, w2, alpha=alpha, tm=tm_p)
        xs = _post(xs, ms, wo, ln_g[i], ln_b[i], wi, w2, alpha=alpha, tm=ns)

    stack = lambda name: jnp.stack(outs[name])
    return (xp.reshape(bp, tp, d), xs.reshape(bs, ts, d),
            stack("fk_p"), stack("fv_p"), stack("flf_p"), stack("hs_p"), stack("pl_p"),
            stack("fk_s"), stack("fv_s"), stack("flf_s"), stack("hs_s"), stack("pl_s"))
```

```python
import functools

import jax
import jax.numpy as jnp
from jax import lax
from jax.experimental import pallas as pl
from jax.experimental.pallas import tpu as pltpu

F32 = jnp.float32
BF16 = jnp.bfloat16

N_MIXERS = 3
POOL_WINDOWS = (2, 4, 8, 16)
LN_EPS = 1e-5
RMS_EPS = 1e-6
LOG2E = 1.4426950408889634

LANES = 128
VMEM_LIMIT_BYTES = 56 * 1024 * 1024
HGRN_CHUNK = 128
HGRN_SUB = 16
HGRN_SAFE_EXPONENT = 80.0
FFN_CHUNK = 256


def _params(n_axes, vmem=VMEM_LIMIT_BYTES):
    return pltpu.CompilerParams(dimension_semantics=("arbitrary",) * n_axes,
                                vmem_limit_bytes=vmem)


def _const_spec(shape):
    nd = len(shape)
    return pl.BlockSpec(shape, lambda *_: (0,) * nd, pipeline_mode=pl.Buffered(1))


def _dot(a, b):
    return jnp.dot(a, b, preferred_element_type=F32)


def _layer_norm(z, g, b):
    mu = jnp.mean(z, axis=-1, keepdims=True)
    zc = z - mu
    var = jnp.mean(zc * zc, axis=-1, keepdims=True)
    return zc * lax.rsqrt(var + LN_EPS) * g + b


def _silu(a):
    return a * jax.nn.sigmoid(a)


def _post_body(*refs, alpha, hidden, has_wo):
    if has_wo:
        x_ref, m_ref, wo_ref, g_ref, b_ref, wi_ref, w2_ref, o_ref, h_ref = refs
        mix = _dot(m_ref[...], wo_ref[...])
    else:
        x_ref, m_ref, g_ref, b_ref, wi_ref, w2_ref, o_ref, h_ref = refs
        mix = m_ref[...]
    x1 = _layer_norm(alpha * x_ref[...] + mix, g_ref[0:1, :], b_ref[0:1, :])
    xb = x1.astype(BF16)
    for c in range(hidden // FFN_CHUNK):
        lo = c * FFN_CHUNK
        a = _dot(xb, wi_ref[:, lo:lo + FFN_CHUNK])
        u = _dot(xb, wi_ref[:, hidden + lo:hidden + lo + FFN_CHUNK])
        h_ref[:, lo:lo + FFN_CHUNK] = (_silu(a) * u).astype(BF16)
    f = _dot(h_ref[...], w2_ref[...])
    o_ref[...] = _layer_norm(alpha * x1 + f, g_ref[1:2, :], b_ref[1:2, :])


def _post(x, m, wo, ln_g, ln_b, wi, w2, *, alpha, tm):
    n, d = x.shape
    hidden = w2.shape[0]
    row = pl.BlockSpec((tm, d), lambda i: (i, 0))
    has_wo = wo is not None
    args = [x, m] + ([wo] if has_wo else []) + [ln_g, ln_b, wi, w2]
    specs = [row, row] + ([_const_spec(wo.shape)] if has_wo else []) + [
        _const_spec(ln_g.shape), _const_spec(ln_b.shape), _const_spec(wi.shape), _const_spec(w2.shape)]
    return pl.pallas_call(
        functools.partial(_post_body, alpha=alpha, hidden=hidden, has_wo=has_wo),
        grid=(n // tm,),
        in_specs=specs,
        out_specs=row,
        out_shape=jax.ShapeDtypeStruct((n, d), F32),
        scratch_shapes=[pltpu.VMEM((tm, hidden), BF16)],
        compiler_params=_params(1),
        name="post_ffn",
    )(*args)


def _log_sigmoid(z):
    return jnp.minimum(z, 0.0) - jnp.log1p(jnp.exp(-jnp.abs(z)))


def _nt(a, b):
    return lax.dot_general(a, b, (((1,), (1,)), ((), ())), preferred_element_type=F32)


def _tn(a, b):
    return lax.dot_general(a, b, (((0,), (0,)), ((), ())), preferred_element_type=F32)


def _fox_proj_t_body(*refs, scale, nh, aliased):
    x_ref, wq_ref, wkvf_ref, bf_ref = refs[:4]
    q_ref, kt_ref, vt_ref, lft_ref = refs[4 + (3 if aliased else 0):]
    d = x_ref.shape[2]
    xb = x_ref[0].astype(BF16)
    q_ref[0] = (_dot(xb, wq_ref[...]) * scale).astype(BF16)
    kvf = _nt(wkvf_ref[...], xb)
    kt_ref[0, 0] = kvf[0:d]
    vt_ref[0, 0] = kvf[d:2 * d]
    lft_ref[0, 0] = _log_sigmoid(kvf[2 * d:2 * d + nh] + bf_ref[...])


def _fox_proj_t(x, wq, wkvf, bf_col, prev, *, layer_slot, n_slots, q_scale, tm):
    bsz, t, d = x.shape
    nh = bf_col.shape[0]
    aliased = prev is not None
    xspec = pl.BlockSpec((1, tm, d), lambda i, j: (i, j, 0))
    kspec = pl.BlockSpec((1, 1, d, tm), lambda i, j: (layer_slot, i, 0, j))
    lspec = pl.BlockSpec((1, 1, nh, tm), lambda i, j: (layer_slot, i, 0, j))
    anyspec = pl.BlockSpec(memory_space=pl.ANY)
    n_in = 4
    return pl.pallas_call(
        functools.partial(_fox_proj_t_body, scale=q_scale, nh=nh, aliased=aliased),
        grid=(bsz, t // tm),
        in_specs=[xspec, _const_spec(wq.shape), _const_spec(wkvf.shape), _const_spec(bf_col.shape)]
                 + ([anyspec] * 3 if aliased else []),
        out_specs=[xspec, kspec, kspec, lspec],
        out_shape=[jax.ShapeDtypeStruct((bsz, t, d), BF16),
                   jax.ShapeDtypeStruct((n_slots, bsz, d, t), F32),
                   jax.ShapeDtypeStruct((n_slots, bsz, d, t), F32),
                   jax.ShapeDtypeStruct((n_slots, bsz, nh, t), F32)],
        input_output_aliases={n_in: 1, n_in + 1: 2, n_in + 2: 3} if aliased else {},
        compiler_params=_params(2),
        name="fox_proj_prompt",
    )(x, wq, wkvf, bf_col, *(prev if aliased else ()))


def _fox_proj_body(x_ref, wq_ref, wk_ref, wv_ref, wf_ref, bf_ref, q_ref, k_ref, v_ref, lf_ref,
                   *, scale, nh):
    xb = x_ref[...].astype(BF16)
    q_ref[...] = (_dot(xb, wq_ref[...]) * scale).astype(BF16)
    k_ref[...] = _dot(xb, wk_ref[...])
    v_ref[...] = _dot(xb, wv_ref[...])
    z = _dot(xb, wf_ref[...]) + bf_ref[...]
    lf_ref[...] = _log_sigmoid(z)[:, :nh]


def _fox_proj(x, wq, wk, wv, wf, bf, *, nh, q_scale, tm):
    n, d = x.shape
    row = pl.BlockSpec((tm, d), lambda i: (i, 0))
    return pl.pallas_call(
        functools.partial(_fox_proj_body, scale=q_scale, nh=nh),
        grid=(n // tm,),
        in_specs=[row, _const_spec(wq.shape), _const_spec(wk.shape), _const_spec(wv.shape),
                  _const_spec(wf.shape), _const_spec(bf.shape)],
        out_specs=[row, row, row, pl.BlockSpec((tm, nh), lambda i: (i, 0))],
        out_shape=[jax.ShapeDtypeStruct((n, d), BF16), jax.ShapeDtypeStruct((n, d), F32),
                   jax.ShapeDtypeStruct((n, d), F32), jax.ShapeDtypeStruct((n, nh), F32)],
        compiler_params=_params(1),
        name="fox_proj_sample",
    )(x, wq, wk, wv, wf, bf)


def _split3(x):
    hi = x.astype(BF16)
    r = x - hi.astype(F32)
    mid = r.astype(BF16)
    lo = (r - mid.astype(F32)).astype(BF16)
    return hi, mid, lo


def _cumsum_body(*refs, has_tail):
    if has_tail:
        lf_ref, tail_ref, c_ref, ctail_ref = refs
    else:
        lf_ref, c_ref = refs
    nh, t = lf_ref.shape[2], lf_ref.shape[3]

    def upper(n):
        r = lax.broadcasted_iota(jnp.int32, (n, n), 0)
        c = lax.broadcasted_iota(jnp.int32, (n, n), 1)
        return (r <= c).astype(BF16)

    def scan(block, tri, carry):
        hi, mid, lo = _split3(block)
        return (_dot(hi, tri) + _dot(mid, tri) + _dot(lo, tri)) + carry

    tri = upper(LANES)
    carry = jnp.zeros((nh, 1), F32)
    for j in range(t // LANES):
        cs = scan(lf_ref[0, 0, :, j * LANES:(j + 1) * LANES], tri, carry)
        c_ref[0, :, j * LANES:(j + 1) * LANES] = cs
        carry = cs[:, LANES - 1:LANES]
    if has_tail:
        ctail_ref[0] = scan(tail_ref[0], upper(tail_ref.shape[2]), carry)


def _cumsum_time(lf_t, slot, tail=None):
    _, b, nh, t = lf_t.shape
    has_tail = tail is not None
    in_specs = [pl.BlockSpec((1, 1, nh, t), lambda i: (slot, i, 0, 0))]
    out_specs = [pl.BlockSpec((1, nh, t), lambda i: (i, 0, 0))]
    out_shape = [jax.ShapeDtypeStruct((b, nh, t), F32)]
    if has_tail:
        tspec = pl.BlockSpec((1,) + tail.shape[1:], lambda i: (i, 0, 0))
        in_specs.append(tspec)
        out_specs.append(tspec)
        out_shape.append(jax.ShapeDtypeStruct(tail.shape, F32))
    return pl.pallas_call(
        functools.partial(_cumsum_body, has_tail=has_tail), grid=(b,),
        in_specs=in_specs, out_specs=out_specs, out_shape=out_shape,
        compiler_params=_params(1), name="fox_cumsum",
    )(lf_t, *((tail,) if has_tail else ()))


def _head_select(hd):
    lane = lax.broadcasted_iota(jnp.int32, (1, 2 * hd), 1)
    return lane < hd


def _one_head(q2, first, h):
    keep = first if h == 0 else jnp.logical_not(first)
    return jnp.where(keep, q2, jnp.zeros_like(q2))


def _split3_f32(x):
    hi = x.astype(BF16).astype(F32)
    r = x - hi
    mid = r.astype(BF16).astype(F32)
    return hi, mid, r - mid


def _attn_prompt_body(q_ref, kt_ref, vt_ref, c_ref, o_ref, kb, vb, *, tq, hd):
    t = q_ref.shape[1]
    w = 2 * hd
    first = _head_select(hd)
    lane = lax.broadcasted_iota(jnp.int32, (1, w), 1)
    cl = c_ref[0, 0] * LOG2E
    kb[0:w, :] = kt_ref[0, 0].astype(BF16)
    kb[w:2 * w, :] = jnp.concatenate(
        list(_split3_f32(-cl)) + [jnp.ones((3, t), F32), jnp.zeros((w - 9, t), F32)], axis=0).astype(BF16)
    vb[...] = vt_ref[0, 0].astype(BF16)
    col = lax.broadcasted_iota(jnp.int32, (2 * tq, tq), 1)
    row = lax.broadcasted_iota(jnp.int32, (2 * tq, tq), 0)
    causal = col <= jnp.where(row >= tq, row - tq, row)
    def scores(q0):
        k1 = q0 + tq
        q2 = q_ref[0, q0:k1, :]
        qa = []
        for h in range(2):
            c_hi, c_mid, c_lo = _split3_f32(cl[h:h + 1, q0:q0 + 1])
            extra = jnp.where((lane == h) | (lane == 2 + h) | (lane == 4 + h), 1.0, 0.0)
            extra = extra + jnp.where(lane == 6, c_hi, 0.0) + jnp.where(lane == 7, c_mid, 0.0) \
                + jnp.where(lane == 8, c_lo, 0.0)
            qa.append(jnp.concatenate([_one_head(q2, first, h),
                                       jnp.broadcast_to(extra.astype(BF16), (tq, w))], axis=1))
        z = _dot(jnp.concatenate(qa, axis=0), kb[:, 0:k1])
        zd = jnp.where(causal, z[:, q0:], -jnp.inf)
        return jnp.concatenate([z[:, :q0], zd], axis=1) if q0 > 0 else zd

    def finish(q0, z):
        k1 = q0 + tq
        m = jnp.max(z, axis=-1, keepdims=True)
        p = jnp.exp2(z - m)
        l = jnp.sum(p, axis=-1, keepdims=True)
        o = _nt(p.astype(BF16), vb[:, 0:k1]) / l
        o_ref[0, q0:k1, :] = jnp.where(first, o[:tq], o[tq:]).astype(BF16)

    starts = list(range(0, t, tq))
    order = starts[1::2] + starts[::2][::-1]
    z = scores(order[0])
    for n, q0 in enumerate(order):
        z_next = scores(order[n + 1]) if n + 1 < len(order) else None
        finish(q0, z)
        z = z_next


def _attention_prompt(q, kt, vt, c, *, slot, tq, hd):
    b, t, d = q.shape
    nhp = d // (2 * hd)
    qspec = pl.BlockSpec((1, t, 2 * hd), lambda i, j: (i, 0, j))
    kspec = pl.BlockSpec((1, 1, 2 * hd, t), lambda i, j: (slot, i, j, 0))
    cspec = pl.BlockSpec((1, 1, 2, t), lambda i, j: (i, j, 0, 0))
    return pl.pallas_call(
        functools.partial(_attn_prompt_body, tq=tq, hd=hd),
        grid=(b, nhp),
        in_specs=[qspec, kspec, kspec, cspec],
        out_specs=qspec,
        out_shape=jax.ShapeDtypeStruct((b, t, d), BF16),
        scratch_shapes=[pltpu.VMEM((4 * hd, t), BF16), pltpu.VMEM((2 * hd, t), BF16)],
        compiler_params=_params(2),
        name="fox_attention_prompt",
    )(q, kt, vt, c)


def _attn_sample_body(q_ref, kc_ref, vc_ref, kn_ref, vn_ref, cc_ref, cn_ref, o_ref, *, hd):
    ts = q_ref.shape[1]
    first = _head_select(hd)
    q2 = q_ref[0]
    kc = kc_ref[0, 0].astype(BF16)
    vc = vc_ref[0, 0].astype(BF16)
    kn = kn_ref[0].astype(BF16)
    vn = vn_ref[0].astype(BF16)
    causal = (lax.broadcasted_iota(jnp.int32, (ts, ts), 1)
              <= lax.broadcasted_iota(jnp.int32, (ts, ts), 0))
    outs = []
    for h in range(2):
        qh = _one_head(q2, first, h)
        cn = cn_ref[0, 0, h:h + 1, :]
        c0 = cn[:, 0:1]
        zc = _dot(qh, kc) + (c0 - cc_ref[0, 0, h:h + 1, :]) * LOG2E
        zn = jnp.where(causal, _nt(qh, kn) + (c0 - cn) * LOG2E, -jnp.inf)
        m = jnp.maximum(jnp.max(zc, axis=-1, keepdims=True), jnp.max(zn, axis=-1, keepdims=True))
        pc = jnp.exp2(zc - m)
        pn = jnp.exp2(zn - m)
        l = jnp.sum(pc, axis=-1, keepdims=True) + jnp.sum(pn, axis=-1, keepdims=True)
        outs.append((_nt(pc.astype(BF16), vc) + _dot(pn.astype(BF16), vn)) / l)
    o_ref[0] = jnp.where(first, outs[0], outs[1]).astype(BF16)


def _attention_sample(q, kc, vc, kn, vn, cc, cn, *, slot, hd):
    b, ts, d = q.shape
    past = kc.shape[3]
    nhp = d // (2 * hd)
    qspec = pl.BlockSpec((1, ts, 2 * hd), lambda i, j: (i, 0, j))
    kspec = pl.BlockSpec((1, 1, 2 * hd, past), lambda i, j: (slot, i, j, 0))
    return pl.pallas_call(
        functools.partial(_attn_sample_body, hd=hd),
        grid=(b, nhp),
        in_specs=[qspec, kspec, kspec, qspec, qspec,
                  pl.BlockSpec((1, 1, 2, past), lambda i, j: (i, j, 0, 0)),
                  pl.BlockSpec((1, 1, 2, ts), lambda i, j: (i, j, 0, 0))],
        out_specs=qspec,
        out_shape=jax.ShapeDtypeStruct((b, ts, d), BF16),
        compiler_params=_params(2),
        name="fox_attention_sample",
    )(q, kc, vc, kn, vn, cc, cn)


def _hgrn_body(x_ref, w_ref, lb_ref, ng_ref, s0_ref, o_ref, sT_ref,
               state, q_s, k_s, b_s, v_s, g_s, qt_s, kh_s, kt_s, vb_s, *, layer, nh, dk, chunk):
    tt, d = x_ref.shape[1], x_ref.shape[2]
    ti = pl.program_id(1)

    @pl.when(ti == 0)
    def _():
        state[...] = s0_ref[0]

    xb = x_ref[0].astype(BF16)
    lbp = lb_ref[...]
    e = jnp.exp(lbp - jnp.max(lbp, axis=0, keepdims=True))
    lb = jnp.sum(e[1:layer + 1, :], axis=0, keepdims=True) / jnp.sum(e, axis=0, keepdims=True)
    fx = _dot(xb, w_ref[:, d:2 * d])
    q_s[...] = _silu(_dot(xb, w_ref[:, 0:d]))
    v = _dot(xb, w_ref[:, 2 * d:3 * d])
    v_s[...] = v
    vb_s[...] = v.astype(BF16)
    g_s[...] = _silu(_dot(xb, w_ref[:, 3 * d:4 * d]))
    k_s[...] = (1.0 - lb) * jax.nn.sigmoid(-fx)
    b = jnp.log(lb + (1.0 - lb) * jax.nn.sigmoid(fx))
    pos = lax.broadcasted_iota(jnp.int32, (tt, 1), 0) % chunk
    sh = 1
    while sh < chunk:
        b = b + jnp.where(pos >= sh, pltpu.roll(b, sh, axis=0), 0.0)
        sh *= 2
    b_s[...] = b
    mid = chunk // 2 - 1
    spread = jnp.zeros((1, 1), F32)
    for c in range(tt // chunk):
        rows = slice(c * chunk, (c + 1) * chunk)
        bc = b[rows]
        dd = bc - bc[mid:mid + 1, :]
        spread = jnp.maximum(spread, jnp.max(jnp.abs(dd), keepdims=True))
        qt_s[rows, :] = (q_s[rows, :] * jnp.exp(dd)).astype(BF16)
        kc = k_s[rows, :]
        kh_s[rows, :] = (kc * jnp.exp(-dd)).astype(BF16)
        kt_s[rows, :] = (kc * jnp.exp(bc[chunk - 1:chunk, :] - bc)).astype(BF16)
    factorable = spread[0, 0] <= HGRN_SAFE_EXPONENT
    ng = ng_ref[...]

    def finish(h, r0, rows, o, st, bl, upd):
        ls = slice(h * dk, (h + 1) * dk)
        state[h] = jnp.exp(bl) * st + upd
        ms = jnp.mean(o * o, axis=-1, keepdims=True)
        on = o * lax.rsqrt(ms + RMS_EPS) * ng * g_s[pl.ds(r0, rows), ls]
        o_ref[0, pl.ds(r0, rows), ls] = on.astype(BF16)

    @pl.when(factorable)
    def _():
        causal = (lax.broadcasted_iota(jnp.int32, (chunk, chunk), 0)
                  >= lax.broadcasted_iota(jnp.int32, (chunk, chunk), 1))

        def step(i, carry):
            r0 = pl.multiple_of(i * chunk, chunk)
            rm = pl.multiple_of(r0 + mid - 7, 8)
            rl = pl.multiple_of(r0 + chunk - 8, 8)
            for h in range(nh):
                ls = slice(h * dk, (h + 1) * dk)
                bm = b_s[pl.ds(rm, 8), ls][7:8, :]
                bl = b_s[pl.ds(rl, 8), ls][7:8, :]
                qt = qt_s[pl.ds(r0, chunk), ls]
                vb = vb_s[pl.ds(r0, chunk), ls]
                st = state[h]
                rhs = jnp.concatenate([(st * jnp.exp(bm)).astype(BF16), kh_s[pl.ds(r0, chunk), ls]], axis=0)
                both = _nt(qt, rhs)
                dv = st.shape[0]
                sc = jnp.where(causal, both[:, dv:], 0.0).astype(BF16)
                o = _dot(sc, vb) + both[:, :dv]
                finish(h, r0, chunk, o, st, bl, _tn(vb, kt_s[pl.ds(r0, chunk), ls]))
            return carry

        lax.fori_loop(0, tt // chunk, step, 0)

    @pl.when(jnp.logical_not(factorable))
    def _():
        sub = min(HGRN_SUB, chunk)
        row = lax.broadcasted_iota(jnp.int32, (sub, 1), 0)

        def step(i, carry):
            r0 = pl.multiple_of(i * sub, sub)
            inside = (r0 % chunk) != 0
            prev = pl.multiple_of(jnp.maximum(r0 - 8, 0), 8)
            for h in range(nh):
                ls = slice(h * dk, (h + 1) * dk)
                qs = q_s[pl.ds(r0, sub), ls]
                ks = k_s[pl.ds(r0, sub), ls]
                vs = v_s[pl.ds(r0, sub), ls]
                base = jnp.where(inside, b_s[pl.ds(prev, 8), ls][7:8, :], 0.0)
                bs = b_s[pl.ds(r0, sub), ls] - base
                bl = bs[sub - 1:sub, :]
                od = jnp.zeros((sub, dk), F32)
                for s in range(sub):
                    dec = jnp.exp(jnp.where(row >= s, bs - bs[s:s + 1, :], -jnp.inf))
                    r = jnp.sum(qs * ks[s:s + 1, :] * dec, axis=-1, keepdims=True)
                    od = od + r * vs[s:s + 1, :]
                st = state[h]
                qt = (qs * jnp.exp(bs)).astype(BF16)
                kt = (ks * jnp.exp(bl - bs)).astype(BF16)
                o = od + _nt(qt, st.astype(BF16))
                finish(h, r0, sub, o, st, bl, _tn(vs.astype(BF16), kt))
            return carry

        lax.fori_loop(0, tt // sub, step, 0)

    @pl.when(ti == pl.num_programs(1) - 1)
    def _():
        sT_ref[0] = state[...]


def _hgrn(x, w, lb, ng, s0_t, *, layer, tt):
    bsz, t, d = x.shape
    nh, dv, dk = s0_t.shape[1:]
    chunk = min(HGRN_CHUNK, tt)
    blk = pl.BlockSpec((1, tt, d), lambda i, j: (i, j, 0))
    sspec = pl.BlockSpec((1, nh, dv, dk), lambda i, j: (i, 0, 0, 0))
    return pl.pallas_call(
        functools.partial(_hgrn_body, layer=layer, nh=nh, dk=dk, chunk=chunk),
        grid=(bsz, t // tt),
        in_specs=[blk, _const_spec(w.shape), _const_spec(lb.shape), _const_spec(ng.shape), sspec],
        out_specs=[blk, sspec],
        out_shape=[jax.ShapeDtypeStruct((bsz, t, d), BF16),
                   jax.ShapeDtypeStruct((bsz, nh, dv, dk), F32)],
        scratch_shapes=([pltpu.VMEM((nh, dv, dk), F32)] + [pltpu.VMEM((tt, d), F32)] * 5
                        + [pltpu.VMEM((tt, d), BF16)] * 4),
        compiler_params=_params(2), name="hgrn_mix",
    )(x, w, lb, ng, s0_t)


def _pool_body(x_ref, halo_ref, hist_ref, w_ref, sc_ref, y_ref, ext, *, pos0, hpad):
    tm = x_ref.shape[1]
    d = x_ref.shape[2]
    ti = pl.program_id(1)
    gw = d // len(POOL_WINDOWS)

    @pl.when(ti == 0)
    def _():
        ext[0:hpad, :] = hist_ref[0]

    @pl.when(ti > 0)
    def _():
        ext[0:hpad, :] = halo_ref[0]

    ext[hpad:hpad + tm, :] = x_ref[0]
    pos = pos0 + ti * tm + lax.broadcasted_iota(jnp.int32, (tm, 1), 0)
    ys = []
    for gi, w in enumerate(POOL_WINDOWS):
        ls = slice(gi * gw, (gi + 1) * gw)
        xg = ext[hpad:hpad + tm, ls]
        ws = xg
        for j in range(1, w):
            ws = ws + ext[hpad - j:hpad - j + tm, ls]
        count = jnp.minimum(pos + 1, w).astype(F32)
        z = ws / count - xg
        ys.append(_dot(z.astype(BF16), w_ref[gi]))
    y_ref[0] = jnp.concatenate(ys, axis=1) * sc_ref[...]


def _pool(x, hist, w, scale, *, pos0, tm):
    bsz, t, d = x.shape
    hpad = hist.shape[1]
    blk = pl.BlockSpec((1, tm, d), lambda i, j: (i, j, 0))
    per = tm // hpad
    halo = pl.BlockSpec((1, hpad, d), lambda i, j: (i, jnp.maximum(j * per - 1, 0), 0))
    return pl.pallas_call(
        functools.partial(_pool_body, pos0=pos0, hpad=hpad),
        grid=(bsz, t // tm),
        in_specs=[blk, halo, pl.BlockSpec((1, hpad, d), lambda i, j: (i, 0, 0)),
                  pl.BlockSpec(w.shape, lambda i, j: (0, 0, 0)),
                  pl.BlockSpec(scale.shape, lambda i, j: (0, 0))],
        out_specs=blk,
        out_shape=jax.ShapeDtypeStruct((bsz, t, d), F32),
        scratch_shapes=[pltpu.VMEM((hpad + tm, d), F32)],
        compiler_params=_params(2), name="pool_mix",
    )(x, x, hist, w, scale)


def kernel(x_prompt, x_sample, cache_fox_k, cache_fox_v, cache_fox_lf, state_hgrn, cache_pool,
           w_fox_in, b_fox_f, w_fox_out, w_hgrn_in, hgrn_lb, hgrn_norm_g, w_hgrn_out,
           w_pool, pool_scale, w_ffn_in, w_ffn_out, ln_g, ln_b):
    bp, tp, d = x_prompt.shape
    bs, ts, _ = x_sample.shape
    depth = w_ffn_in.shape[0]
    nh = b_fox_f.shape[1]
    hd = d // nh
    past = cache_fox_k.shape[2]
    hist = cache_pool.shape[2]
    hg_heads, hg_dk, hg_dv = state_hgrn.shape[2:]
    alpha = (2 * depth) ** 0.25
    np_, ns = bp * tp, bs * ts
    tm_p = 512
    tq = 256
    hpad = 16

    xp = x_prompt.reshape(np_, d)
    xs = x_sample.reshape(ns, d)
    outs = {name: [] for name in ("hs_p", "pl_p", "fk_s", "fv_s", "flf_s", "hs_s", "pl_s")}
    n_fox = w_fox_in.shape[0]
    q_scale = LOG2E * hd ** -0.5
    fox_prompt = None
    cache_kt = jnp.transpose(cache_fox_k, (0, 1, 3, 4, 2)).reshape(n_fox, bs, d, past)
    cache_vt = jnp.transpose(cache_fox_v, (0, 1, 3, 4, 2)).reshape(n_fox, bs, d, past)
    cache_lft = jnp.swapaxes(cache_fox_lf, 2, 3)

    for i in range(depth):
        j = i // N_MIXERS
        kind = i % N_MIXERS
        wi = w_ffn_in[i].astype(BF16)
        w2 = w_ffn_out[i].astype(BF16)
        wo = None
        if kind == 0:
            w_in = w_fox_in[j]
            wq = w_in[:, :d].astype(BF16)
            wk = w_in[:, d:2 * d].astype(BF16)
            wv = w_in[:, 2 * d:3 * d].astype(BF16)
            wf = jnp.pad(w_in[:, 3 * d:], ((0, 0), (0, LANES - nh))).astype(BF16)
            wkvf = jnp.swapaxes(jnp.pad(w_in[:, d:], ((0, 0), (0, LANES - nh))), 0, 1).astype(BF16)
            bf = jnp.pad(b_fox_f[j], (0, LANES - nh)).reshape(1, LANES)
            wo = w_fox_out[j].astype(BF16)

            q, kt, vt, lft = _fox_proj_t(xp.reshape(bp, tp, d), wq, wkvf, b_fox_f[j].reshape(nh, 1),
                                         fox_prompt, layer_slot=j, n_slots=n_fox, q_scale=q_scale, tm=tm_p)
            fox_prompt = (kt, vt, lft)
            c, = _cumsum_time(lft, j)
            mp = _attention_prompt(q, kt, vt, c.reshape(bp, nh // 2, 2, tp), slot=j, tq=tq, hd=hd)
            mp = mp.reshape(np_, d)

            q, k, v, lf = _fox_proj(xs, wq, wk, wv, wf, bf, nh=nh, q_scale=q_scale, tm=ns)
            cc, cn = _cumsum_time(cache_lft, j, jnp.swapaxes(lf.reshape(bs, ts, nh), 1, 2))
            ms = _attention_sample(q.reshape(bs, ts, d), cache_kt, cache_vt,
                                   k.reshape(bs, ts, d), v.reshape(bs, ts, d),
                                   cc.reshape(bs, nh // 2, 2, past), cn.reshape(bs, nh // 2, 2, ts),
                                   slot=j, hd=hd)
            ms = ms.reshape(ns, d)
            outs["fk_s"].append(k.reshape(bs, ts, nh, hd))
            outs["fv_s"].append(v.reshape(bs, ts, nh, hd))
            outs["flf_s"].append(lf.reshape(bs, ts, nh))
        elif kind == 1:
            w_in = w_hgrn_in[j].astype(BF16)
            wo = w_hgrn_out[j].astype(BF16)
            ng = hgrn_norm_g[j].reshape(1, hg_dv)
            s0 = jnp.zeros((bp, hg_heads, hg_dv, hg_dk), F32)
            mp, st = _hgrn(xp.reshape(bp, tp, d), w_in, hgrn_lb, ng, s0, layer=i, tt=tm_p)
            mp = mp.reshape(np_, d)
            outs["hs_p"].append(jnp.swapaxes(st, 2, 3))
            ms, st = _hgrn(xs.reshape(bs, ts, d), w_in, hgrn_lb, ng,
                           jnp.swapaxes(state_hgrn[j], 2, 3), layer=i, tt=ts)
            ms = ms.reshape(ns, d)
            outs["hs_s"].append(jnp.swapaxes(st, 2, 3))
        else:
            wp = w_pool[j].astype(BF16)
            sc = pool_scale[j].reshape(1, d)
            xp3 = xp.reshape(bp, tp, d)
            xs3 = xs.reshape(bs, ts, d)
            hist_p = jnp.zeros((bp, hpad, d), F32)
            hist_s = jnp.pad(cache_pool[j], ((0, 0), (hpad - hist, 0), (0, 0)))
            mp = _pool(xp3, hist_p, wp, sc, pos0=0, tm=tm_p).reshape(np_, d)
            ms = _pool(xs3, hist_s, wp, sc, pos0=past, tm=ts).reshape(ns, d)
            outs["pl_p"].append(xp3[:, tp - hist:])
            outs["pl_s"].append(jnp.concatenate([cache_pool[j], xs3], axis=1)[:, ts:])
        xp = _post(xp, mp, wo, ln_g[i], ln_b[i], wi, w2, alpha=alpha, tm=tm_p)
        xs = _post(xs, ms, wo, ln_g[i], ln_b[i], wi, w2, alpha=alpha, tm=ns)

    stack = lambda name: jnp.stack(outs[name])
    kt, vt, lft = fox_prompt
    time_major = lambda a: jnp.transpose(a.reshape(n_fox, bp, nh, hd, tp), (0, 1, 4, 2, 3))
    return (xp.reshape(bp, tp, d), xs.reshape(bs, ts, d),
            time_major(kt), time_major(vt), jnp.swapaxes(lft, 2, 3), stack("hs_p"), stack("pl_p"),
            stack("fk_s"), stack("fv_s"), stack("flf_s"), stack("hs_s"), stack("pl_s"))
```

```python
import functools

import jax
import jax.numpy as jnp
from jax import lax
from jax.experimental import pallas as pl
from jax.experimental.pallas import tpu as pltpu

F32 = jnp.float32
BF16 = jnp.bfloat16

N_MIXERS = 3
POOL_WINDOWS = (2, 4, 8, 16)
LN_EPS = 1e-5
RMS_EPS = 1e-6
LOG2E = 1.4426950408889634

LANES = 128
VMEM_LIMIT_BYTES = 56 * 1024 * 1024
HGRN_CHUNK = 128
HGRN_SUB = 16
HGRN_SAFE_EXPONENT = 80.0
FFN_CHUNK = 256
POST_SUB = 512


def _params(n_axes, vmem=VMEM_LIMIT_BYTES):
    return pltpu.CompilerParams(dimension_semantics=("arbitrary",) * n_axes,
                                vmem_limit_bytes=vmem)


def _const_spec(shape):
    nd = len(shape)
    return pl.BlockSpec(shape, lambda *_: (0,) * nd, pipeline_mode=pl.Buffered(1))


def _dot(a, b):
    return jnp.dot(a, b, preferred_element_type=F32)


def _layer_norm(z, g, b):
    mu = jnp.mean(z, axis=-1, keepdims=True)
    zc = z - mu
    var = jnp.mean(zc * zc, axis=-1, keepdims=True)
    return zc * lax.rsqrt(var + LN_EPS) * g + b


def _silu(a):
    return a * jax.nn.sigmoid(a)


def _post_body(*refs, alpha, hidden, has_wo, sub):
    if has_wo:
        x_ref, m_ref, wo_ref, g_ref, b_ref, wi_ref, w2_ref, o_ref, h_ref, x1_ref = refs
    else:
        x_ref, m_ref, g_ref, b_ref, wi_ref, w2_ref, o_ref, h_ref, x1_ref = refs
    groups = [slice(r, r + sub) for r in range(0, x_ref.shape[0], sub)]
    for rs in groups:
        mix = _dot(m_ref[rs, :], wo_ref[...]) if has_wo else m_ref[rs, :]
        x1_ref[rs, :] = _layer_norm(alpha * x_ref[rs, :] + mix, g_ref[0:1, :], b_ref[0:1, :])
    for rs in groups:
        xb = x1_ref[rs, :].astype(BF16)
        for c in range(hidden // FFN_CHUNK):
            lo = c * FFN_CHUNK
            a = _dot(xb, wi_ref[:, lo:lo + FFN_CHUNK])
            u = _dot(xb, wi_ref[:, hidden + lo:hidden + lo + FFN_CHUNK])
            h_ref[rs, lo:lo + FFN_CHUNK] = (_silu(a) * u).astype(BF16)
    for rs in groups:
        f = _dot(h_ref[rs, :], w2_ref[...])
        o_ref[rs, :] = _layer_norm(alpha * x1_ref[rs, :] + f, g_ref[1:2, :], b_ref[1:2, :])


def _post(x, m, wo, ln_g, ln_b, wi, w2, *, alpha, tm):
    n, d = x.shape
    hidden = w2.shape[0]
    row = pl.BlockSpec((tm, d), lambda i: (i, 0))
    has_wo = wo is not None
    args = [x, m] + ([wo] if has_wo else []) + [ln_g, ln_b, wi, w2]
    specs = [row, row] + ([_const_spec(wo.shape)] if has_wo else []) + [
        _const_spec(ln_g.shape), _const_spec(ln_b.shape), _const_spec(wi.shape), _const_spec(w2.shape)]
    return pl.pallas_call(
        functools.partial(_post_body, alpha=alpha, hidden=hidden, has_wo=has_wo, sub=min(tm, POST_SUB)),
        grid=(n // tm,),
        in_specs=specs,
        out_specs=row,
        out_shape=jax.ShapeDtypeStruct((n, d), F32),
        scratch_shapes=[pltpu.VMEM((tm, hidden), BF16), pltpu.VMEM((tm, d), F32)],
        compiler_params=_params(1),
        name="post_ffn",
    )(*args)


def _log_sigmoid(z):
    return jnp.minimum(z, 0.0) - jnp.log1p(jnp.exp(-jnp.abs(z)))


def _nt(a, b):
    return lax.dot_general(a, b, (((1,), (1,)), ((), ())), preferred_element_type=F32)


def _tn(a, b):
    return lax.dot_general(a, b, (((0,), (0,)), ((), ())), preferred_element_type=F32)


def _fox_proj_t_body(*refs, scale, nh, aliased):
    x_ref, wq_ref, wkvf_ref, bf_ref = refs[:4]
    q_ref, kt_ref, vt_ref, lft_ref = refs[4 + (3 if aliased else 0):]
    d = x_ref.shape[2]
    xb = x_ref[0].astype(BF16)
    q_ref[0] = (_dot(xb, wq_ref[...]) * scale).astype(BF16)
    kvf = _nt(wkvf_ref[...], xb)
    kt_ref[0, 0] = kvf[0:d]
    vt_ref[0, 0] = kvf[d:2 * d]
    lft_ref[0, 0] = _log_sigmoid(kvf[2 * d:2 * d + nh] + bf_ref[...])


def _fox_proj_t(x, wq, wkvf, bf_col, prev, *, layer_slot, n_slots, q_scale, tm):
    bsz, t, d = x.shape
    nh = bf_col.shape[0]
    aliased = prev is not None
    xspec = pl.BlockSpec((1, tm, d), lambda i, j: (i, j, 0))
    kspec = pl.BlockSpec((1, 1, d, tm), lambda i, j: (layer_slot, i, 0, j))
    lspec = pl.BlockSpec((1, 1, nh, tm), lambda i, j: (layer_slot, i, 0, j))
    anyspec = pl.BlockSpec(memory_space=pl.ANY)
    n_in = 4
    return pl.pallas_call(
        functools.partial(_fox_proj_t_body, scale=q_scale, nh=nh, aliased=aliased),
        grid=(bsz, t // tm),
        in_specs=[xspec, _const_spec(wq.shape), _const_spec(wkvf.shape), _const_spec(bf_col.shape)]
                 + ([anyspec] * 3 if aliased else []),
        out_specs=[xspec, kspec, kspec, lspec],
        out_shape=[jax.ShapeDtypeStruct((bsz, t, d), BF16),
                   jax.ShapeDtypeStruct((n_slots, bsz, d, t), F32),
                   jax.ShapeDtypeStruct((n_slots, bsz, d, t), F32),
                   jax.ShapeDtypeStruct((n_slots, bsz, nh, t), F32)],
        input_output_aliases={n_in: 1, n_in + 1: 2, n_in + 2: 3} if aliased else {},
        compiler_params=_params(2),
        name="fox_proj_prompt",
    )(x, wq, wkvf, bf_col, *(prev if aliased else ()))


def _fox_proj_body(x_ref, wq_ref, wk_ref, wv_ref, wf_ref, bf_ref, q_ref, k_ref, v_ref, lf_ref,
                   *, scale, nh):
    xb = x_ref[...].astype(BF16)
    q_ref[...] = (_dot(xb, wq_ref[...]) * scale).astype(BF16)
    k_ref[...] = _dot(xb, wk_ref[...])
    v_ref[...] = _dot(xb, wv_ref[...])
    z = _dot(xb, wf_ref[...]) + bf_ref[...]
    lf_ref[...] = _log_sigmoid(z)[:, :nh]


def _fox_proj(x, wq, wk, wv, wf, bf, *, nh, q_scale, tm):
    n, d = x.shape
    row = pl.BlockSpec((tm, d), lambda i: (i, 0))
    return pl.pallas_call(
        functools.partial(_fox_proj_body, scale=q_scale, nh=nh),
        grid=(n // tm,),
        in_specs=[row, _const_spec(wq.shape), _const_spec(wk.shape), _const_spec(wv.shape),
                  _const_spec(wf.shape), _const_spec(bf.shape)],
        out_specs=[row, row, row, pl.BlockSpec((tm, nh), lambda i: (i, 0))],
        out_shape=[jax.ShapeDtypeStruct((n, d), BF16), jax.ShapeDtypeStruct((n, d), F32),
                   jax.ShapeDtypeStruct((n, d), F32), jax.ShapeDtypeStruct((n, nh), F32)],
        compiler_params=_params(1),
        name="fox_proj_sample",
    )(x, wq, wk, wv, wf, bf)


def _split3(x):
    hi = x.astype(BF16)
    r = x - hi.astype(F32)
    mid = r.astype(BF16)
    lo = (r - mid.astype(F32)).astype(BF16)
    return hi, mid, lo


def _cumsum_body(*refs, has_tail):
    if has_tail:
        lf_ref, tail_ref, c_ref, ctail_ref = refs
    else:
        lf_ref, c_ref = refs
    nh, t = lf_ref.shape[2], lf_ref.shape[3]

    def upper(n):
        r = lax.broadcasted_iota(jnp.int32, (n, n), 0)
        c = lax.broadcasted_iota(jnp.int32, (n, n), 1)
        return (r <= c).astype(BF16)

    def scan(block, tri, carry):
        hi, mid, lo = _split3(block)
        return (_dot(hi, tri) + _dot(mid, tri) + _dot(lo, tri)) + carry

    tri = upper(LANES)
    carry = jnp.zeros((nh, 1), F32)
    for j in range(t // LANES):
        cs = scan(lf_ref[0, 0, :, j * LANES:(j + 1) * LANES], tri, carry)
        c_ref[0, :, j * LANES:(j + 1) * LANES] = cs
        carry = cs[:, LANES - 1:LANES]
    if has_tail:
        ctail_ref[0] = scan(tail_ref[0], upper(tail_ref.shape[2]), carry)


def _cumsum_time(lf_t, slot, tail=None):
    _, b, nh, t = lf_t.shape
    has_tail = tail is not None
    in_specs = [pl.BlockSpec((1, 1, nh, t), lambda i: (slot, i, 0, 0))]
    out_specs = [pl.BlockSpec((1, nh, t), lambda i: (i, 0, 0))]
    out_shape = [jax.ShapeDtypeStruct((b, nh, t), F32)]
    if has_tail:
        tspec = pl.BlockSpec((1,) + tail.shape[1:], lambda i: (i, 0, 0))
        in_specs.append(tspec)
        out_specs.append(tspec)
        out_shape.append(jax.ShapeDtypeStruct(tail.shape, F32))
    return pl.pallas_call(
        functools.partial(_cumsum_body, has_tail=has_tail), grid=(b,),
        in_specs=in_specs, out_specs=out_specs, out_shape=out_shape,
        compiler_params=_params(1), name="fox_cumsum",
    )(lf_t, *((tail,) if has_tail else ()))


def _head_select(hd):
    lane = lax.broadcasted_iota(jnp.int32, (1, 2 * hd), 1)
    return lane < hd


def _one_head(q2, first, h):
    keep = first if h == 0 else jnp.logical_not(first)
    return jnp.where(keep, q2, jnp.zeros_like(q2))


def _split3_f32(x):
    hi = x.astype(BF16).astype(F32)
    r = x - hi
    mid = r.astype(BF16).astype(F32)
    return hi, mid, r - mid


def _attn_prompt_body(q_ref, kt_ref, vt_ref, c_ref, o_ref, kb, vb, *, tq, hd):
    t = q_ref.shape[1]
    w = 2 * hd
    first = _head_select(hd)
    lane = lax.broadcasted_iota(jnp.int32, (1, w), 1)
    cl = c_ref[0, 0] * LOG2E
    kb[0:w, :] = kt_ref[0, 0].astype(BF16)
    kb[w:2 * w, :] = jnp.concatenate(
        list(_split3_f32(-cl)) + [jnp.ones((3, t), F32), jnp.zeros((w - 9, t), F32)], axis=0).astype(BF16)
    vb[...] = vt_ref[0, 0].astype(BF16)
    col = lax.broadcasted_iota(jnp.int32, (2 * tq, tq), 1)
    row = lax.broadcasted_iota(jnp.int32, (2 * tq, tq), 0)
    causal = col <= jnp.where(row >= tq, row - tq, row)
    def scores(q0):
        k1 = q0 + tq
        q2 = q_ref[0, q0:k1, :]
        qa = []
        for h in range(2):
            c_hi, c_mid, c_lo = _split3_f32(cl[h:h + 1, q0:q0 + 1])
            extra = jnp.where((lane == h) | (lane == 2 + h) | (lane == 4 + h), 1.0, 0.0)
            extra = extra + jnp.where(lane == 6, c_hi, 0.0) + jnp.where(lane == 7, c_mid, 0.0) \
                + jnp.where(lane == 8, c_lo, 0.0)
            qa.append(jnp.concatenate([_one_head(q2, first, h),
                                       jnp.broadcast_to(extra.astype(BF16), (tq, w))], axis=1))
        z = _dot(jnp.concatenate(qa, axis=0), kb[:, 0:k1])
        zd = jnp.where(causal, z[:, q0:], -jnp.inf)
        return jnp.concatenate([z[:, :q0], zd], axis=1) if q0 > 0 else zd

    def finish(q0, z):
        k1 = q0 + tq
        m = jnp.max(z, axis=-1, keepdims=True)
        p = jnp.exp2(z - m)
        l = jnp.sum(p, axis=-1, keepdims=True)
        o = _nt(p.astype(BF16), vb[:, 0:k1]) / l
        o_ref[0, q0:k1, :] = jnp.where(first, o[:tq], o[tq:]).astype(BF16)

    starts = list(range(0, t, tq))
    order = starts[1::2] + starts[::2][::-1]
    z = scores(order[0])
    for n, q0 in enumerate(order):
        z_next = scores(order[n + 1]) if n + 1 < len(order) else None
        finish(q0, z)
        z = z_next


def _attention_prompt(q, kt, vt, c, *, slot, tq, hd):
    b, t, d = q.shape
    nhp = d // (2 * hd)
    qspec = pl.BlockSpec((1, t, 2 * hd), lambda i, j: (i, 0, j))
    kspec = pl.BlockSpec((1, 1, 2 * hd, t), lambda i, j: (slot, i, j, 0))
    cspec = pl.BlockSpec((1, 1, 2, t), lambda i, j: (i, j, 0, 0))
    return pl.pallas_call(
        functools.partial(_attn_prompt_body, tq=tq, hd=hd),
        grid=(b, nhp),
        in_specs=[qspec, kspec, kspec, cspec],
        out_specs=qspec,
        out_shape=jax.ShapeDtypeStruct((b, t, d), BF16),
        scratch_shapes=[pltpu.VMEM((4 * hd, t), BF16), pltpu.VMEM((2 * hd, t), BF16)],
        compiler_params=_params(2),
        name="fox_attention_prompt",
    )(q, kt, vt, c)


def _attn_sample_body(q_ref, kc_ref, vc_ref, kn_ref, vn_ref, cc_ref, cn_ref, o_ref, *, hd):
    ts = q_ref.shape[1]
    first = _head_select(hd)
    q2 = q_ref[0]
    kc = kc_ref[0, 0].astype(BF16)
    vc = vc_ref[0, 0].astype(BF16)
    kn = kn_ref[0].astype(BF16)
    vn = vn_ref[0].astype(BF16)
    row = lax.broadcasted_iota(jnp.int32, (2 * ts, ts), 0)
    causal = lax.broadcasted_iota(jnp.int32, (2 * ts, ts), 1) <= jnp.where(row >= ts, row - ts, row)
    qs = jnp.concatenate([_one_head(q2, first, 0), _one_head(q2, first, 1)], axis=0)

    def bias(c_ref):
        c = c_ref[0, 0]
        c0 = cn_ref[0, 0][:, 0:1]
        b = (c0 - c) * LOG2E
        return jnp.concatenate([jnp.broadcast_to(b[0:1], (ts, b.shape[1])),
                                jnp.broadcast_to(b[1:2], (ts, b.shape[1]))], axis=0)

    zc = _dot(qs, kc) + bias(cc_ref)
    zn = jnp.where(causal, _nt(qs, kn) + bias(cn_ref), -jnp.inf)
    m = jnp.maximum(jnp.max(zc, axis=-1, keepdims=True), jnp.max(zn, axis=-1, keepdims=True))
    pc = jnp.exp2(zc - m)
    pn = jnp.exp2(zn - m)
    l = jnp.sum(pc, axis=-1, keepdims=True) + jnp.sum(pn, axis=-1, keepdims=True)
    o = (_nt(pc.astype(BF16), vc) + _dot(pn.astype(BF16), vn)) / l
    o_ref[0] = jnp.where(first, o[:ts], o[ts:]).astype(BF16)


def _attention_sample(q, kc, vc, kn, vn, cc, cn, *, slot, hd):
    b, ts, d = q.shape
    past = kc.shape[3]
    nhp = d // (2 * hd)
    qspec = pl.BlockSpec((1, ts, 2 * hd), lambda i, j: (i, 0, j))
    kspec = pl.BlockSpec((1, 1, 2 * hd, past), lambda i, j: (slot, i, j, 0))
    return pl.pallas_call(
        functools.partial(_attn_sample_body, hd=hd),
        grid=(b, nhp),
        in_specs=[qspec, kspec, kspec, qspec, qspec,
                  pl.BlockSpec((1, 1, 2, past), lambda i, j: (i, j, 0, 0)),
                  pl.BlockSpec((1, 1, 2, ts), lambda i, j: (i, j, 0, 0))],
        out_specs=qspec,
        out_shape=jax.ShapeDtypeStruct((b, ts, d), BF16),
        compiler_params=_params(2),
        name="fox_attention_sample",
    )(q, kc, vc, kn, vn, cc, cn)


def _hgrn_body(x_ref, w_ref, lb_ref, ng_ref, s0_ref, o_ref, sT_ref,
               state, q_s, k_s, b_s, v_s, g_s, qt_s, kh_s, kt_s, vb_s, *, layer, nh, dk, chunk):
    tt, d = x_ref.shape[1], x_ref.shape[2]
    ti = pl.program_id(1)

    @pl.when(ti == 0)
    def _():
        state[...] = s0_ref[0]

    xb = x_ref[0].astype(BF16)
    lbp = lb_ref[...]
    e = jnp.exp(lbp - jnp.max(lbp, axis=0, keepdims=True))
    lb = jnp.sum(e[1:layer + 1, :], axis=0, keepdims=True) / jnp.sum(e, axis=0, keepdims=True)
    fx = _dot(xb, w_ref[:, d:2 * d])
    q_s[...] = _silu(_dot(xb, w_ref[:, 0:d]))
    v = _dot(xb, w_ref[:, 2 * d:3 * d])
    v_s[...] = v
    vb_s[...] = v.astype(BF16)
    g_s[...] = _silu(_dot(xb, w_ref[:, 3 * d:4 * d]))
    k_s[...] = (1.0 - lb) * jax.nn.sigmoid(-fx)
    g = jnp.log(lb + (1.0 - lb) * jax.nn.sigmoid(fx))
    tri = (lax.broadcasted_iota(jnp.int32, (chunk, chunk), 0)
           >= lax.broadcasted_iota(jnp.int32, (chunk, chunk), 1)).astype(BF16)
    tri3 = jnp.concatenate([tri, tri, tri], axis=1)
    mid = chunk // 2 - 1
    spread = jnp.zeros((1, 1), F32)
    for c in range(tt // chunk):
        rows = slice(c * chunk, (c + 1) * chunk)
        bc = _dot(tri3, jnp.concatenate(list(_split3(g[rows])), axis=0))
        b_s[rows, :] = bc
        dd = bc - bc[mid:mid + 1, :]
        spread = jnp.maximum(spread, jnp.max(jnp.abs(dd), keepdims=True))
        qt_s[rows, :] = (q_s[rows, :] * jnp.exp(dd)).astype(BF16)
        kc = k_s[rows, :]
        kh_s[rows, :] = (kc * jnp.exp(-dd)).astype(BF16)
        kt_s[rows, :] = (kc * jnp.exp(bc[chunk - 1:chunk, :] - bc)).astype(BF16)
    factorable = spread[0, 0] <= HGRN_SAFE_EXPONENT
    ng = ng_ref[...]

    def finish(h, r0, rows, o, st, bl, upd):
        ls = slice(h * dk, (h + 1) * dk)
        state[h] = jnp.exp(bl) * st + upd
        ms = jnp.mean(o * o, axis=-1, keepdims=True)
        on = o * lax.rsqrt(ms + RMS_EPS) * ng * g_s[pl.ds(r0, rows), ls]
        o_ref[0, pl.ds(r0, rows), ls] = on.astype(BF16)

    @pl.when(factorable)
    def _():
        causal = (lax.broadcasted_iota(jnp.int32, (chunk, chunk), 0)
                  >= lax.broadcasted_iota(jnp.int32, (chunk, chunk), 1))

        for r0 in range(0, tt, chunk):
            rm = r0 + mid - 7
            rl = r0 + chunk - 8
            for h in range(nh):
                ls = slice(h * dk, (h + 1) * dk)
                bm = b_s[pl.ds(rm, 8), ls][7:8, :]
                bl = b_s[pl.ds(rl, 8), ls][7:8, :]
                qt = qt_s[pl.ds(r0, chunk), ls]
                vb = vb_s[pl.ds(r0, chunk), ls]
                st = state[h]
                rhs = jnp.concatenate([(st * jnp.exp(bm)).astype(BF16), kh_s[pl.ds(r0, chunk), ls]], axis=0)
                both = _nt(qt, rhs)
                dv = st.shape[0]
                sc = jnp.where(causal, both[:, dv:], 0.0).astype(BF16)
                o = _dot(sc, vb) + both[:, :dv]
                finish(h, r0, chunk, o, st, bl, _tn(vb, kt_s[pl.ds(r0, chunk), ls]))

    @pl.when(jnp.logical_not(factorable))
    def _():
        sub = min(HGRN_SUB, chunk)
        row = lax.broadcasted_iota(jnp.int32, (sub, 1), 0)

        def step(i, carry):
            r0 = pl.multiple_of(i * sub, sub)
            inside = (r0 % chunk) != 0
            prev = pl.multiple_of(jnp.maximum(r0 - 8, 0), 8)
            for h in range(nh):
                ls = slice(h * dk, (h + 1) * dk)
                qs = q_s[pl.ds(r0, sub), ls]
                ks = k_s[pl.ds(r0, sub), ls]
                vs = v_s[pl.ds(r0, sub), ls]
                base = jnp.where(inside, b_s[pl.ds(prev, 8), ls][7:8, :], 0.0)
                bs = b_s[pl.ds(r0, sub), ls] - base
                bl = bs[sub - 1:sub, :]
                od = jnp.zeros((sub, dk), F32)
                for s in range(sub):
                    dec = jnp.exp(jnp.where(row >= s, bs - bs[s:s + 1, :], -jnp.inf))
                    r = jnp.sum(qs * ks[s:s + 1, :] * dec, axis=-1, keepdims=True)
                    od = od + r * vs[s:s + 1, :]
                st = state[h]
                qt = (qs * jnp.exp(bs)).astype(BF16)
                kt = (ks * jnp.exp(bl - bs)).astype(BF16)
                o = od + _nt(qt, st.astype(BF16))
                finish(h, r0, sub, o, st, bl, _tn(vs.astype(BF16), kt))
            return carry

        lax.fori_loop(0, tt // sub, step, 0)

    @pl.when(ti == pl.num_programs(1) - 1)
    def _():
        sT_ref[0] = state[...]


def _hgrn(x, w, lb, ng, s0_t, *, layer, tt):
    bsz, t, d = x.shape
    nh, dv, dk = s0_t.shape[1:]
    chunk = min(HGRN_CHUNK, tt)
    blk = pl.BlockSpec((1, tt, d), lambda i, j: (i, j, 0))
    sspec = pl.BlockSpec((1, nh, dv, dk), lambda i, j: (i, 0, 0, 0))
    return pl.pallas_call(
        functools.partial(_hgrn_body, layer=layer, nh=nh, dk=dk, chunk=chunk),
        grid=(bsz, t // tt),
        in_specs=[blk, _const_spec(w.shape), _const_spec(lb.shape), _const_spec(ng.shape), sspec],
        out_specs=[blk, sspec],
        out_shape=[jax.ShapeDtypeStruct((bsz, t, d), BF16),
                   jax.ShapeDtypeStruct((bsz, nh, dv, dk), F32)],
        scratch_shapes=([pltpu.VMEM((nh, dv, dk), F32)] + [pltpu.VMEM((tt, d), F32)] * 5
                        + [pltpu.VMEM((tt, d), BF16)] * 4),
        compiler_params=_params(2), name="hgrn_mix",
    )(x, w, lb, ng, s0_t)


def _pool_body(x_ref, halo_ref, hist_ref, w_ref, sc_ref, y_ref, ext, *, pos0, hpad):
    tm = x_ref.shape[1]
    d = x_ref.shape[2]
    ti = pl.program_id(1)
    gw = d // len(POOL_WINDOWS)

    @pl.when(ti == 0)
    def _():
        ext[0:hpad, :] = hist_ref[0]

    @pl.when(ti > 0)
    def _():
        ext[0:hpad, :] = halo_ref[0]

    ext[hpad:hpad + tm, :] = x_ref[0]
    pos = pos0 + ti * tm + lax.broadcasted_iota(jnp.int32, (tm, 1), 0)
    ys = []
    for gi, w in enumerate(POOL_WINDOWS):
        ls = slice(gi * gw, (gi + 1) * gw)
        acc = ext[:, ls]
        xg = acc[hpad:, :]
        sh = 1
        while sh < w:
            acc = acc + pltpu.roll(acc, sh, axis=0)
            sh *= 2
        count = jnp.minimum(pos + 1, w).astype(F32)
        z = acc[hpad:, :] / count - xg
        ys.append(_dot(z.astype(BF16), w_ref[gi]))
    y_ref[0] = jnp.concatenate(ys, axis=1) * sc_ref[...]


def _pool(x, hist, w, scale, *, pos0, tm):
    bsz, t, d = x.shape
    hpad = hist.shape[1]
    blk = pl.BlockSpec((1, tm, d), lambda i, j: (i, j, 0))
    per = tm // hpad
    halo = pl.BlockSpec((1, hpad, d), lambda i, j: (i, jnp.maximum(j * per - 1, 0), 0))
    return pl.pallas_call(
        functools.partial(_pool_body, pos0=pos0, hpad=hpad),
        grid=(bsz, t // tm),
        in_specs=[blk, halo, pl.BlockSpec((1, hpad, d), lambda i, j: (i, 0, 0)),
                  pl.BlockSpec(w.shape, lambda i, j: (0, 0, 0)),
                  pl.BlockSpec(scale.shape, lambda i, j: (0, 0))],
        out_specs=blk,
        out_shape=jax.ShapeDtypeStruct((bsz, t, d), F32),
        scratch_shapes=[pltpu.VMEM((hpad + tm, d), F32)],
        compiler_params=_params(2), name="pool_mix",
    )(x, x, hist, w, scale)


def kernel(x_prompt, x_sample, cache_fox_k, cache_fox_v, cache_fox_lf, state_hgrn, cache_pool,
           w_fox_in, b_fox_f, w_fox_out, w_hgrn_in, hgrn_lb, hgrn_norm_g, w_hgrn_out,
           w_pool, pool_scale, w_ffn_in, w_ffn_out, ln_g, ln_b):
    bp, tp, d = x_prompt.shape
    bs, ts, _ = x_sample.shape
    depth = w_ffn_in.shape[0]
    nh = b_fox_f.shape[1]
    hd = d // nh
    past = cache_fox_k.shape[2]
    hist = cache_pool.shape[2]
    hg_heads, hg_dk, hg_dv = state_hgrn.shape[2:]
    alpha = (2 * depth) ** 0.25
    np_, ns = bp * tp, bs * ts
    tm_p = 512
    tq = 256
    hpad = 16

    xp = x_prompt.reshape(np_, d)
    xs = x_sample.reshape(ns, d)
    outs = {name: [] for name in ("hs_p", "pl_p", "fk_s", "fv_s", "flf_s", "hs_s", "pl_s")}
    n_fox = w_fox_in.shape[0]
    q_scale = LOG2E * hd ** -0.5
    fox_prompt = None
    cache_kt = jnp.transpose(cache_fox_k, (0, 1, 3, 4, 2)).reshape(n_fox, bs, d, past)
    cache_vt = jnp.transpose(cache_fox_v, (0, 1, 3, 4, 2)).reshape(n_fox, bs, d, past)
    cache_lft = jnp.swapaxes(cache_fox_lf, 2, 3)

    for i in range(depth):
        j = i // N_MIXERS
        kind = i % N_MIXERS
        wi = w_ffn_in[i].astype(BF16)
        w2 = w_ffn_out[i].astype(BF16)
        wo = None
        if kind == 0:
            w_in = w_fox_in[j]
            wq = w_in[:, :d].astype(BF16)
            wk = w_in[:, d:2 * d].astype(BF16)
            wv = w_in[:, 2 * d:3 * d].astype(BF16)
            wf = jnp.pad(w_in[:, 3 * d:], ((0, 0), (0, LANES - nh))).astype(BF16)
            wkvf = jnp.swapaxes(jnp.pad(w_in[:, d:], ((0, 0), (0, LANES - nh))), 0, 1).astype(BF16)
            bf = jnp.pad(b_fox_f[j], (0, LANES - nh)).reshape(1, LANES)
            wo = w_fox_out[j].astype(BF16)

            q, kt, vt, lft = _fox_proj_t(xp.reshape(bp, tp, d), wq, wkvf, b_fox_f[j].reshape(nh, 1),
                                         fox_prompt, layer_slot=j, n_slots=n_fox, q_scale=q_scale, tm=tm_p)
            fox_prompt = (kt, vt, lft)
            c, = _cumsum_time(lft, j)
            mp = _attention_prompt(q, kt, vt, c.reshape(bp, nh // 2, 2, tp), slot=j, tq=tq, hd=hd)
            mp = mp.reshape(np_, d)

            q, k, v, lf = _fox_proj(xs, wq, wk, wv, wf, bf, nh=nh, q_scale=q_scale, tm=ns)
            cc, cn = _cumsum_time(cache_lft, j, jnp.swapaxes(lf.reshape(bs, ts, nh), 1, 2))
            ms = _attention_sample(q.reshape(bs, ts, d), cache_kt, cache_vt,
                                   k.reshape(bs, ts, d), v.reshape(bs, ts, d),
                                   cc.reshape(bs, nh // 2, 2, past), cn.reshape(bs, nh // 2, 2, ts),
                                   slot=j, hd=hd)
            ms = ms.reshape(ns, d)
            outs["fk_s"].append(k.reshape(bs, ts, nh, hd))
            outs["fv_s"].append(v.reshape(bs, ts, nh, hd))
            outs["flf_s"].append(lf.reshape(bs, ts, nh))
        elif kind == 1:
            w_in = w_hgrn_in[j].astype(BF16)
            wo = w_hgrn_out[j].astype(BF16)
            ng = hgrn_norm_g[j].reshape(1, hg_dv)
            s0 = jnp.zeros((bp, hg_heads, hg_dv, hg_dk), F32)
            mp, st = _hgrn(xp.reshape(bp, tp, d), w_in, hgrn_lb, ng, s0, layer=i, tt=tm_p)
            mp = mp.reshape(np_, d)
            outs["hs_p"].append(jnp.swapaxes(st, 2, 3))
            ms, st = _hgrn(xs.reshape(bs, ts, d), w_in, hgrn_lb, ng,
                           jnp.swapaxes(state_hgrn[j], 2, 3), layer=i, tt=ts)
            ms = ms.reshape(ns, d)
            outs["hs_s"].append(jnp.swapaxes(st, 2, 3))
        else:
            wp = w_pool[j].astype(BF16)
            sc = pool_scale[j].reshape(1, d)
            xp3 = xp.reshape(bp, tp, d)
            xs3 = xs.reshape(bs, ts, d)
            hist_p = jnp.zeros((bp, hpad, d), F32)
            hist_s = jnp.pad(cache_pool[j], ((0, 0), (hpad - hist, 0), (0, 0)))
            mp = _pool(xp3, hist_p, wp, sc, pos0=0, tm=tm_p).reshape(np_, d)
            ms = _pool(xs3, hist_s, wp, sc, pos0=past, tm=ts).reshape(ns, d)
            outs["pl_p"].append(xp3[:, tp - hist:])
            outs["pl_s"].append(jnp.concatenate([cache_pool[j], xs3], axis=1)[:, ts:])
        xp = _post(xp, mp, wo, ln_g[i], ln_b[i], wi, w2, alpha=alpha, tm=2 * POST_SUB)
        xs = _post(xs, ms, wo, ln_g[i], ln_b[i], wi, w2, alpha=alpha, tm=ns)

    stack = lambda name: jnp.stack(outs[name])
    kt, vt, lft = fox_prompt
    time_major = lambda a: jnp.transpose(a.reshape(n_fox, bp, nh, hd, tp), (0, 1, 4, 2, 3))
    return (xp.reshape(bp, tp, d), xs.reshape(bs, ts, d),
            time_major(kt), time_major(vt), jnp.swapaxes(lft, 2, 3), stack("hs_p"), stack("pl_p"),
            stack("fk_s"), stack("fv_s"), stack("flf_s"), stack("hs_s"), stack("pl_s"))
```

```python
import functools

import jax
import jax.numpy as jnp
from jax import lax
from jax.experimental import pallas as pl
from jax.experimental.pallas import tpu as pltpu

F32 = jnp.float32
BF16 = jnp.bfloat16

N_MIXERS = 3
POOL_WINDOWS = (2, 4, 8, 16)
LN_EPS = 1e-5
RMS_EPS = 1e-6
LOG2E = 1.4426950408889634

LANES = 128
VMEM_LIMIT_BYTES = 56 * 1024 * 1024
HGRN_CHUNK = 128
HGRN_SUB = 16
HGRN_SAFE_EXPONENT = 80.0
FFN_CHUNK = 256
POST_SUB = 512
ATTN_PAIRS = 2


def _params(n_axes, vmem=VMEM_LIMIT_BYTES):
    return pltpu.CompilerParams(dimension_semantics=("arbitrary",) * n_axes,
                                vmem_limit_bytes=vmem)


def _const_spec(shape):
    nd = len(shape)
    return pl.BlockSpec(shape, lambda *_: (0,) * nd, pipeline_mode=pl.Buffered(1))


def _dot(a, b):
    return jnp.dot(a, b, preferred_element_type=F32)


def _layer_norm(z, g, b):
    mu = jnp.mean(z, axis=-1, keepdims=True)
    zc = z - mu
    var = jnp.mean(zc * zc, axis=-1, keepdims=True)
    return zc * lax.rsqrt(var + LN_EPS) * g + b


def _silu(a):
    return a * jax.nn.sigmoid(a)


def _post_body(*refs, alpha, hidden, has_wo, sub):
    if has_wo:
        x_ref, m_ref, wo_ref, g_ref, b_ref, wi_ref, w2_ref, o_ref, h_ref, x1_ref = refs
    else:
        x_ref, m_ref, g_ref, b_ref, wi_ref, w2_ref, o_ref, h_ref, x1_ref = refs
    groups = [slice(r, r + sub) for r in range(0, x_ref.shape[0], sub)]
    for rs in groups:
        mix = _dot(m_ref[rs, :], wo_ref[0]) if has_wo else m_ref[rs, :]
        x1_ref[rs, :] = _layer_norm(alpha * x_ref[rs, :] + mix, g_ref[0, 0:1, :], b_ref[0, 0:1, :])
    for rs in groups:
        xb = x1_ref[rs, :].astype(BF16)
        for c in range(hidden // FFN_CHUNK):
            lo = c * FFN_CHUNK
            a = _dot(xb, wi_ref[0, :, lo:lo + FFN_CHUNK])
            u = _dot(xb, wi_ref[0, :, hidden + lo:hidden + lo + FFN_CHUNK])
            h_ref[rs, lo:lo + FFN_CHUNK] = (_silu(a) * u).astype(BF16)
    for rs in groups:
        f = _dot(h_ref[rs, :], w2_ref[0])
        o_ref[rs, :] = _layer_norm(alpha * x1_ref[rs, :] + f, g_ref[0, 1:2, :], b_ref[0, 1:2, :])


def _layer_spec(shape, layer):
    rest = (0,) * (len(shape) - 1)
    return pl.BlockSpec((1,) + tuple(shape[1:]), lambda *_: (layer,) + rest, pipeline_mode=pl.Buffered(1))


def _post(x, m, wo, wo_layer, ln_g, ln_b, wi, w2, layer, *, alpha, tm):
    n, d = x.shape
    hidden = w2.shape[1]
    row = pl.BlockSpec((tm, d), lambda i: (i, 0))
    has_wo = wo is not None
    args = [x, m] + ([wo] if has_wo else []) + [ln_g, ln_b, wi, w2]
    specs = [row, row] + ([_layer_spec(wo.shape, wo_layer)] if has_wo else []) + [
        _layer_spec(ln_g.shape, layer), _layer_spec(ln_b.shape, layer),
        _layer_spec(wi.shape, layer), _layer_spec(w2.shape, layer)]
    return pl.pallas_call(
        functools.partial(_post_body, alpha=alpha, hidden=hidden, has_wo=has_wo, sub=min(tm, POST_SUB)),
        grid=(n // tm,),
        in_specs=specs,
        out_specs=row,
        out_shape=jax.ShapeDtypeStruct((n, d), F32),
        scratch_shapes=[pltpu.VMEM((tm, hidden), BF16), pltpu.VMEM((tm, d), F32)],
        compiler_params=_params(1),
        name="post_ffn",
    )(*args)


def _log_sigmoid(z):
    return jnp.minimum(z, 0.0) - jnp.log1p(jnp.exp(-jnp.abs(z)))


def _nt(a, b):
    return lax.dot_general(a, b, (((1,), (1,)), ((), ())), preferred_element_type=F32)


def _tn(a, b):
    return lax.dot_general(a, b, (((0,), (0,)), ((), ())), preferred_element_type=F32)


def _fox_proj_t_body(*refs, scale, nh, aliased, own):
    x_ref, wq_ref, wkvf_ref, bf_ref = refs[:4]
    q_ref, kt_ref, vt_ref, lft_ref = refs[4 + (3 if aliased else 0):]
    d = x_ref.shape[2]
    xb = x_ref[0].astype(BF16)
    q_ref[0] = (_dot(xb, wq_ref[...]) * scale).astype(BF16)
    kvf = _nt(wkvf_ref[...], xb)
    kt_ref[own, 0] = kvf[0:d]
    vt_ref[own, 0] = kvf[d:2 * d]
    lft_ref[own, 0] = _log_sigmoid(kvf[2 * d:2 * d + nh] + bf_ref[...])
    for ref in (kt_ref, vt_ref, lft_ref):
        for s in range(ref.shape[0]):
            if s != own:
                ref[s, 0] = jnp.zeros(ref.shape[2:], F32)


def _fox_proj_t(x, wq, wkvf, bf_col, prev, *, layer_slot, n_slots, q_scale, tm):
    bsz, t, d = x.shape
    nh = bf_col.shape[0]
    aliased = prev is not None
    xspec = pl.BlockSpec((1, tm, d), lambda i, j: (i, j, 0))
    if aliased:
        kspec = pl.BlockSpec((1, 1, d, tm), lambda i, j: (layer_slot, i, 0, j))
        lspec = pl.BlockSpec((1, 1, nh, tm), lambda i, j: (layer_slot, i, 0, j))
    else:
        kspec = pl.BlockSpec((n_slots, 1, d, tm), lambda i, j: (0, i, 0, j))
        lspec = pl.BlockSpec((n_slots, 1, nh, tm), lambda i, j: (0, i, 0, j))
    anyspec = pl.BlockSpec(memory_space=pl.ANY)
    n_in = 4
    return pl.pallas_call(
        functools.partial(_fox_proj_t_body, scale=q_scale, nh=nh, aliased=aliased,
                          own=0 if aliased else layer_slot),
        grid=(bsz, t // tm),
        in_specs=[xspec, _const_spec(wq.shape), _const_spec(wkvf.shape), _const_spec(bf_col.shape)]
                 + ([anyspec] * 3 if aliased else []),
        out_specs=[xspec, kspec, kspec, lspec],
        out_shape=[jax.ShapeDtypeStruct((bsz, t, d), BF16),
                   jax.ShapeDtypeStruct((n_slots, bsz, d, t), F32),
                   jax.ShapeDtypeStruct((n_slots, bsz, d, t), F32),
                   jax.ShapeDtypeStruct((n_slots, bsz, nh, t), F32)],
        input_output_aliases={n_in: 1, n_in + 1: 2, n_in + 2: 3} if aliased else {},
        compiler_params=_params(2),
        name="fox_proj_prompt",
    )(x, wq, wkvf, bf_col, *(prev if aliased else ()))


def _fox_proj_body(x_ref, wq_ref, wk_ref, wv_ref, wf_ref, bf_ref, q_ref, k_ref, v_ref, lf_ref,
                   *, scale, nh):
    xb = x_ref[...].astype(BF16)
    q_ref[...] = (_dot(xb, wq_ref[...]) * scale).astype(BF16)
    k_ref[...] = _dot(xb, wk_ref[...])
    v_ref[...] = _dot(xb, wv_ref[...])
    z = _dot(xb, wf_ref[...]) + bf_ref[...]
    lf_ref[...] = _log_sigmoid(z)[:, :nh]


def _fox_proj(x, wq, wk, wv, wf, bf, *, nh, q_scale, tm):
    n, d = x.shape
    row = pl.BlockSpec((tm, d), lambda i: (i, 0))
    return pl.pallas_call(
        functools.partial(_fox_proj_body, scale=q_scale, nh=nh),
        grid=(n // tm,),
        in_specs=[row, _const_spec(wq.shape), _const_spec(wk.shape), _const_spec(wv.shape),
                  _const_spec(wf.shape), _const_spec(bf.shape)],
        out_specs=[row, row, row, pl.BlockSpec((tm, nh), lambda i: (i, 0))],
        out_shape=[jax.ShapeDtypeStruct((n, d), BF16), jax.ShapeDtypeStruct((n, d), F32),
                   jax.ShapeDtypeStruct((n, d), F32), jax.ShapeDtypeStruct((n, nh), F32)],
        compiler_params=_params(1),
        name="fox_proj_sample",
    )(x, wq, wk, wv, wf, bf)


def _split3(x):
    hi = x.astype(BF16)
    r = x - hi.astype(F32)
    mid = r.astype(BF16)
    lo = (r - mid.astype(F32)).astype(BF16)
    return hi, mid, lo


def _cumsum_body(*refs, has_tail):
    if has_tail:
        lf_ref, tail_ref, c_ref, ctail_ref = refs
    else:
        lf_ref, c_ref = refs
    nh, t = lf_ref.shape[2], lf_ref.shape[3]

    def upper(n):
        r = lax.broadcasted_iota(jnp.int32, (n, n), 0)
        c = lax.broadcasted_iota(jnp.int32, (n, n), 1)
        return (r <= c).astype(BF16)

    def scan(block, tri):
        return _dot(jnp.concatenate(list(_split3(block)), axis=1), jnp.concatenate([tri] * 3, axis=0))

    nblk = t // LANES
    rows = nblk * nh
    stacked = jnp.concatenate([lf_ref[0, 0, :, j * LANES:(j + 1) * LANES] for j in range(nblk)], axis=0)
    local = scan(stacked, upper(LANES))
    r = lax.broadcasted_iota(jnp.int32, (rows, rows), 0)
    c = lax.broadcasted_iota(jnp.int32, (rows, rows), 1)
    earlier = ((c < r) & ((r - c) % nh == 0)).astype(BF16)
    before = _dot(jnp.concatenate([earlier] * 3, axis=1), jnp.concatenate(list(_split3(stacked)), axis=0))
    total = local + jnp.sum(before, axis=-1, keepdims=True)
    for j in range(nblk):
        c_ref[0, :, j * LANES:(j + 1) * LANES] = total[j * nh:(j + 1) * nh]
    if has_tail:
        ctail_ref[0] = scan(tail_ref[0], upper(tail_ref.shape[2])) + total[rows - nh:rows, LANES - 1:LANES]


def _cumsum_time(lf_t, slot, tail=None):
    _, b, nh, t = lf_t.shape
    has_tail = tail is not None
    in_specs = [pl.BlockSpec((1, 1, nh, t), lambda i: (slot, i, 0, 0))]
    out_specs = [pl.BlockSpec((1, nh, t), lambda i: (i, 0, 0))]
    out_shape = [jax.ShapeDtypeStruct((b, nh, t), F32)]
    if has_tail:
        tspec = pl.BlockSpec((1,) + tail.shape[1:], lambda i: (i, 0, 0))
        in_specs.append(tspec)
        out_specs.append(tspec)
        out_shape.append(jax.ShapeDtypeStruct(tail.shape, F32))
    return pl.pallas_call(
        functools.partial(_cumsum_body, has_tail=has_tail), grid=(b,),
        in_specs=in_specs, out_specs=out_specs, out_shape=out_shape,
        compiler_params=_params(1), name="fox_cumsum",
    )(lf_t, *((tail,) if has_tail else ()))


def _head_select(hd):
    lane = lax.broadcasted_iota(jnp.int32, (1, 2 * hd), 1)
    return lane < hd


def _one_head(q2, first, h):
    keep = first if h == 0 else jnp.logical_not(first)
    return jnp.where(keep, q2, jnp.zeros_like(q2))


def _split3_f32(x):
    hi = x.astype(BF16).astype(F32)
    r = x - hi
    mid = r.astype(BF16).astype(F32)
    return hi, mid, r - mid


def _attn_prompt_body(q_ref, kt_ref, vt_ref, c_ref, o_ref, kb, vb, *, tq, hd):
    t = q_ref.shape[1]
    w = 2 * hd
    pairs = q_ref.shape[2] // w
    first = _head_select(hd)
    lane = lax.broadcasted_iota(jnp.int32, (1, w), 1)
    cls = []
    for pp in range(pairs):
        cl = c_ref[0, pp] * LOG2E
        cls.append(cl)
        kb[pp, 0:w, :] = kt_ref[0, 0, pp * w:(pp + 1) * w, :].astype(BF16)
        kb[pp, w:2 * w, :] = jnp.concatenate(
            list(_split3_f32(-cl)) + [jnp.ones((3, t), F32), jnp.zeros((w - 9, t), F32)], axis=0).astype(BF16)
        vb[pp] = vt_ref[0, 0, pp * w:(pp + 1) * w, :].astype(BF16)
    col = lax.broadcasted_iota(jnp.int32, (2 * tq, tq), 1)
    row = lax.broadcasted_iota(jnp.int32, (2 * tq, tq), 0)
    causal = col <= jnp.where(row >= tq, row - tq, row)

    def scores(pp, q0):
        k1 = q0 + tq
        q2 = q_ref[0, q0:k1, pp * w:(pp + 1) * w]
        qa = []
        for h in range(2):
            c_hi, c_mid, c_lo = _split3_f32(cls[pp][h:h + 1, q0:q0 + 1])
            extra = jnp.where((lane == h) | (lane == 2 + h) | (lane == 4 + h), 1.0, 0.0)
            extra = extra + jnp.where(lane == 6, c_hi, 0.0) + jnp.where(lane == 7, c_mid, 0.0) \
                + jnp.where(lane == 8, c_lo, 0.0)
            qa.append(jnp.concatenate([_one_head(q2, first, h),
                                       jnp.broadcast_to(extra.astype(BF16), (tq, w))], axis=1))
        z = _dot(jnp.concatenate(qa, axis=0), kb[pp, :, 0:k1])
        zd = jnp.where(causal, z[:, q0:], -jnp.inf)
        return jnp.concatenate([z[:, :q0], zd], axis=1) if q0 > 0 else zd

    def finish(pp, q0, z):
        k1 = q0 + tq
        m = jnp.max(z, axis=-1, keepdims=True)
        p = jnp.exp2(z - m)
        l = jnp.sum(p, axis=-1, keepdims=True)
        o = _nt(p.astype(BF16), vb[pp, :, 0:k1]) / l
        o_ref[0, q0:k1, pp * w:(pp + 1) * w] = jnp.where(first, o[:tq], o[tq:]).astype(BF16)

    starts = list(range(0, t, tq))
    order = [(pp, q0) for q0 in starts[1::2] + starts[::2][::-1] for pp in range(pairs)]
    z = scores(*order[0])
    for n, item in enumerate(order):
        z_next = scores(*order[n + 1]) if n + 1 < len(order) else None
        finish(*item, z)
        z = z_next


def _attention_prompt(q, kt, vt, c, *, slot, tq, hd, pairs):
    b, t, d = q.shape
    w = 2 * hd * pairs
    qspec = pl.BlockSpec((1, t, w), lambda i, j: (i, 0, j))
    kspec = pl.BlockSpec((1, 1, w, t), lambda i, j: (slot, i, j, 0))
    cspec = pl.BlockSpec((1, pairs, 2, t), lambda i, j: (i, j, 0, 0))
    return pl.pallas_call(
        functools.partial(_attn_prompt_body, tq=tq, hd=hd),
        grid=(b, d // w),
        in_specs=[qspec, kspec, kspec, cspec],
        out_specs=qspec,
        out_shape=jax.ShapeDtypeStruct((b, t, d), BF16),
        scratch_shapes=[pltpu.VMEM((pairs, 4 * hd, t), BF16), pltpu.VMEM((pairs, 2 * hd, t), BF16)],
        compiler_params=_params(2),
        name="fox_attention_prompt",
    )(q, kt, vt, c)


def _attn_sample_body(q_ref, kc_ref, vc_ref, kn_ref, vn_ref, cc_ref, cn_ref, o_ref, *, hd):
    ts = q_ref.shape[1]
    first = _head_select(hd)
    q2 = q_ref[0]
    kc = kc_ref[0, 0].astype(BF16)
    vc = vc_ref[0, 0].astype(BF16)
    kn = kn_ref[0].astype(BF16)
    vn = vn_ref[0].astype(BF16)
    row = lax.broadcasted_iota(jnp.int32, (2 * ts, ts), 0)
    causal = lax.broadcasted_iota(jnp.int32, (2 * ts, ts), 1) <= jnp.where(row >= ts, row - ts, row)
    qs = jnp.concatenate([_one_head(q2, first, 0), _one_head(q2, first, 1)], axis=0)

    def bias(c_ref):
        c = c_ref[0, 0]
        c0 = cn_ref[0, 0][:, 0:1]
        b = (c0 - c) * LOG2E
        return jnp.concatenate([jnp.broadcast_to(b[0:1], (ts, b.shape[1])),
                                jnp.broadcast_to(b[1:2], (ts, b.shape[1]))], axis=0)

    zc = _dot(qs, kc) + bias(cc_ref)
    zn = jnp.where(causal, _nt(qs, kn) + bias(cn_ref), -jnp.inf)
    m = jnp.maximum(jnp.max(zc, axis=-1, keepdims=True), jnp.max(zn, axis=-1, keepdims=True))
    pc = jnp.exp2(zc - m)
    pn = jnp.exp2(zn - m)
    l = jnp.sum(pc, axis=-1, keepdims=True) + jnp.sum(pn, axis=-1, keepdims=True)
    o = (_nt(pc.astype(BF16), vc) + _dot(pn.astype(BF16), vn)) / l
    o_ref[0] = jnp.where(first, o[:ts], o[ts:]).astype(BF16)


def _attention_sample(q, kc, vc, kn, vn, cc, cn, *, slot, hd):
    b, ts, d = q.shape
    past = kc.shape[3]
    nhp = d // (2 * hd)
    qspec = pl.BlockSpec((1, ts, 2 * hd), lambda i, j: (i, 0, j))
    kspec = pl.BlockSpec((1, 1, 2 * hd, past), lambda i, j: (slot, i, j, 0))
    return pl.pallas_call(
        functools.partial(_attn_sample_body, hd=hd),
        grid=(b, nhp),
        in_specs=[qspec, kspec, kspec, qspec, qspec,
                  pl.BlockSpec((1, 1, 2, past), lambda i, j: (i, j, 0, 0)),
                  pl.BlockSpec((1, 1, 2, ts), lambda i, j: (i, j, 0, 0))],
        out_specs=qspec,
        out_shape=jax.ShapeDtypeStruct((b, ts, d), BF16),
        compiler_params=_params(2),
        name="fox_attention_sample",
    )(q, kc, vc, kn, vn, cc, cn)


def _hgrn_body(x_ref, w_ref, lb_ref, ng_ref, s0_ref, o_ref, sT_ref,
               state, q_s, k_s, b_s, v_s, g_s, qt_s, kh_s, kt_s, vb_s, *, layer, nh, dk, chunk):
    tt, d = x_ref.shape[1], x_ref.shape[2]
    ti = pl.program_id(1)

    @pl.when(ti == 0)
    def _():
        state[...] = s0_ref[0]

    xb = x_ref[0].astype(BF16)
    lbp = lb_ref[...]
    e = jnp.exp(lbp - jnp.max(lbp, axis=0, keepdims=True))
    lb = jnp.sum(e[1:layer + 1, :], axis=0, keepdims=True) / jnp.sum(e, axis=0, keepdims=True)
    fx = _dot(xb, w_ref[:, d:2 * d])
    q_s[...] = _silu(_dot(xb, w_ref[:, 0:d]))
    v = _dot(xb, w_ref[:, 2 * d:3 * d])
    v_s[...] = v
    vb_s[...] = v.astype(BF16)
    g_s[...] = _silu(_dot(xb, w_ref[:, 3 * d:4 * d]))
    k_s[...] = (1.0 - lb) * jax.nn.sigmoid(-fx)
    g = jnp.log(lb + (1.0 - lb) * jax.nn.sigmoid(fx))
    tri = (lax.broadcasted_iota(jnp.int32, (chunk, chunk), 0)
           >= lax.broadcasted_iota(jnp.int32, (chunk, chunk), 1)).astype(BF16)
    tri3 = jnp.concatenate([tri, tri, tri], axis=1)
    mid = chunk // 2 - 1
    spread = jnp.zeros((1, 1), F32)
    for c in range(tt // chunk):
        rows = slice(c * chunk, (c + 1) * chunk)
        bc = _dot(tri3, jnp.concatenate(list(_split3(g[rows])), axis=0))
        b_s[rows, :] = bc
        dd = bc - bc[mid:mid + 1, :]
        spread = jnp.maximum(spread, jnp.max(jnp.abs(dd), keepdims=True))
        qt_s[rows, :] = (q_s[rows, :] * jnp.exp(dd)).astype(BF16)
        kc = k_s[rows, :]
        kh_s[rows, :] = (kc * jnp.exp(-dd)).astype(BF16)
        kt_s[rows, :] = (kc * jnp.exp(bc[chunk - 1:chunk, :] - bc)).astype(BF16)
    factorable = spread[0, 0] <= HGRN_SAFE_EXPONENT
    ng = ng_ref[...]

    def finish(h, r0, rows, o, st, bl, upd):
        ls = slice(h * dk, (h + 1) * dk)
        state[h] = jnp.exp(bl) * st + upd
        ms = jnp.mean(o * o, axis=-1, keepdims=True)
        on = o * lax.rsqrt(ms + RMS_EPS) * ng * g_s[pl.ds(r0, rows), ls]
        o_ref[0, pl.ds(r0, rows), ls] = on.astype(BF16)

    @pl.when(factorable)
    def _():
        causal = (lax.broadcasted_iota(jnp.int32, (chunk, chunk), 0)
                  >= lax.broadcasted_iota(jnp.int32, (chunk, chunk), 1))

        for r0 in range(0, tt, chunk):
            rm = r0 + mid - 7
            rl = r0 + chunk - 8
            for h in range(nh):
                ls = slice(h * dk, (h + 1) * dk)
                bm = b_s[pl.ds(rm, 8), ls][7:8, :]
                bl = b_s[pl.ds(rl, 8), ls][7:8, :]
                qt = qt_s[pl.ds(r0, chunk), ls]
                vb = vb_s[pl.ds(r0, chunk), ls]
                st = state[h]
                rhs = jnp.concatenate([(st * jnp.exp(bm)).astype(BF16), kh_s[pl.ds(r0, chunk), ls]], axis=0)
                both = _nt(qt, rhs)
                dv = st.shape[0]
                sc = jnp.where(causal, both[:, dv:], 0.0).astype(BF16)
                o = _dot(sc, vb) + both[:, :dv]
                finish(h, r0, chunk, o, st, bl, _tn(vb, kt_s[pl.ds(r0, chunk), ls]))

    @pl.when(jnp.logical_not(factorable))
    def _():
        sub = min(HGRN_SUB, chunk)
        row = lax.broadcasted_iota(jnp.int32, (sub, 1), 0)

        def step(i, carry):
            r0 = pl.multiple_of(i * sub, sub)
            inside = (r0 % chunk) != 0
            prev = pl.multiple_of(jnp.maximum(r0 - 8, 0), 8)
            for h in range(nh):
                ls = slice(h * dk, (h + 1) * dk)
                qs = q_s[pl.ds(r0, sub), ls]
                ks = k_s[pl.ds(r0, sub), ls]
                vs = v_s[pl.ds(r0, sub), ls]
                base = jnp.where(inside, b_s[pl.ds(prev, 8), ls][7:8, :], 0.0)
                bs = b_s[pl.ds(r0, sub), ls] - base
                bl = bs[sub - 1:sub, :]
                od = jnp.zeros((sub, dk), F32)
                for s in range(sub):
                    dec = jnp.exp(jnp.where(row >= s, bs - bs[s:s + 1, :], -jnp.inf))
                    r = jnp.sum(qs * ks[s:s + 1, :] * dec, axis=-1, keepdims=True)
                    od = od + r * vs[s:s + 1, :]
                st = state[h]
                qt = (qs * jnp.exp(bs)).astype(BF16)
                kt = (ks * jnp.exp(bl - bs)).astype(BF16)
                o = od + _nt(qt, st.astype(BF16))
                finish(h, r0, sub, o, st, bl, _tn(vs.astype(BF16), kt))
            return carry

        lax.fori_loop(0, tt // sub, step, 0)

    @pl.when(ti == pl.num_programs(1) - 1)
    def _():
        sT_ref[0] = state[...]


def _hgrn(x, w, lb, ng, s0_t, *, layer, tt):
    bsz, t, d = x.shape
    nh, dv, dk = s0_t.shape[1:]
    chunk = min(HGRN_CHUNK, tt)
    blk = pl.BlockSpec((1, tt, d), lambda i, j: (i, j, 0))
    sspec = pl.BlockSpec((1, nh, dv, dk), lambda i, j: (i, 0, 0, 0))
    return pl.pallas_call(
        functools.partial(_hgrn_body, layer=layer, nh=nh, dk=dk, chunk=chunk),
        grid=(bsz, t // tt),
        in_specs=[blk, _const_spec(w.shape), _const_spec(lb.shape), _const_spec(ng.shape), sspec],
        out_specs=[blk, sspec],
        out_shape=[jax.ShapeDtypeStruct((bsz, t, d), BF16),
                   jax.ShapeDtypeStruct((bsz, nh, dv, dk), F32)],
        scratch_shapes=([pltpu.VMEM((nh, dv, dk), F32)] + [pltpu.VMEM((tt, d), F32)] * 5
                        + [pltpu.VMEM((tt, d), BF16)] * 4),
        compiler_params=_params(2), name="hgrn_mix",
    )(x, w, lb, ng, s0_t)


def _pool_body(x_ref, halo_ref, hist_ref, w_ref, sc_ref, y_ref, ext, *, pos0, hpad):
    tm = x_ref.shape[1]
    d = x_ref.shape[2]
    ti = pl.program_id(1)
    gw = d // len(POOL_WINDOWS)

    @pl.when(ti == 0)
    def _():
        ext[0:hpad, :] = hist_ref[0]

    @pl.when(ti > 0)
    def _():
        ext[0:hpad, :] = halo_ref[0]

    ext[hpad:hpad + tm, :] = x_ref[0]
    pos = pos0 + ti * tm + lax.broadcasted_iota(jnp.int32, (tm, 1), 0)
    ys = []
    for gi, w in enumerate(POOL_WINDOWS):
        ls = slice(gi * gw, (gi + 1) * gw)
        acc = ext[:, ls]
        xg = acc[hpad:, :]
        sh = 1
        while sh < w:
            acc = acc + pltpu.roll(acc, sh, axis=0)
            sh *= 2
        count = jnp.minimum(pos + 1, w).astype(F32)
        z = acc[hpad:, :] / count - xg
        ys.append(_dot(z.astype(BF16), w_ref[gi]))
    y_ref[0] = jnp.concatenate(ys, axis=1) * sc_ref[...]


def _pool(x, hist, w, scale, *, pos0, tm):
    bsz, t, d = x.shape
    hpad = hist.shape[1]
    blk = pl.BlockSpec((1, tm, d), lambda i, j: (i, j, 0))
    per = tm // hpad
    halo = pl.BlockSpec((1, hpad, d), lambda i, j: (i, jnp.maximum(j * per - 1, 0), 0))
    return pl.pallas_call(
        functools.partial(_pool_body, pos0=pos0, hpad=hpad),
        grid=(bsz, t // tm),
        in_specs=[blk, halo, pl.BlockSpec((1, hpad, d), lambda i, j: (i, 0, 0)),
                  pl.BlockSpec(w.shape, lambda i, j: (0, 0, 0)),
                  pl.BlockSpec(scale.shape, lambda i, j: (0, 0))],
        out_specs=blk,
        out_shape=jax.ShapeDtypeStruct((bsz, t, d), F32),
        scratch_shapes=[pltpu.VMEM((hpad + tm, d), F32)],
        compiler_params=_params(2), name="pool_mix",
    )(x, x, hist, w, scale)


def kernel(x_prompt, x_sample, cache_fox_k, cache_fox_v, cache_fox_lf, state_hgrn, cache_pool,
           w_fox_in, b_fox_f, w_fox_out, w_hgrn_in, hgrn_lb, hgrn_norm_g, w_hgrn_out,
           w_pool, pool_scale, w_ffn_in, w_ffn_out, ln_g, ln_b):
    bp, tp, d = x_prompt.shape
    bs, ts, _ = x_sample.shape
    depth = w_ffn_in.shape[0]
    nh = b_fox_f.shape[1]
    hd = d // nh
    past = cache_fox_k.shape[2]
    hist = cache_pool.shape[2]
    hg_heads, hg_dk, hg_dv = state_hgrn.shape[2:]
    alpha = (2 * depth) ** 0.25
    np_, ns = bp * tp, bs * ts
    tm_p = 512
    tq = 256
    hpad = 16

    xp = x_prompt.reshape(np_, d)
    xs = x_sample.reshape(ns, d)
    outs = {name: [] for name in ("hs_p", "pl_p", "fk_s", "fv_s", "flf_s", "hs_s", "pl_s")}
    n_fox = w_fox_in.shape[0]
    q_scale = LOG2E * hd ** -0.5
    fox_prompt = None
    cache_kt = jnp.transpose(cache_fox_k, (0, 1, 3, 4, 2)).reshape(n_fox, bs, d, past)
    cache_vt = jnp.transpose(cache_fox_v, (0, 1, 3, 4, 2)).reshape(n_fox, bs, d, past)
    cache_lft = jnp.swapaxes(cache_fox_lf, 2, 3)
    wi = w_ffn_in.astype(BF16)
    w2 = w_ffn_out.astype(BF16)
    w_fox_out_b = w_fox_out.astype(BF16)
    w_hgrn_out_b = w_hgrn_out.astype(BF16)

    for i in range(depth):
        j = i // N_MIXERS
        kind = i % N_MIXERS
        wo = None
        if kind == 0:
            w_in = w_fox_in[j]
            wq = w_in[:, :d].astype(BF16)
            wk = w_in[:, d:2 * d].astype(BF16)
            wv = w_in[:, 2 * d:3 * d].astype(BF16)
            wf = jnp.pad(w_in[:, 3 * d:], ((0, 0), (0, LANES - nh))).astype(BF16)
            wkvf = jnp.swapaxes(jnp.pad(w_in[:, d:], ((0, 0), (0, LANES - nh))), 0, 1).astype(BF16)
            bf = jnp.pad(b_fox_f[j], (0, LANES - nh)).reshape(1, LANES)
            wo = w_fox_out_b

            q, kt, vt, lft = _fox_proj_t(xp.reshape(bp, tp, d), wq, wkvf, b_fox_f[j].reshape(nh, 1),
                                         fox_prompt, layer_slot=j, n_slots=n_fox, q_scale=q_scale, tm=tm_p)
            fox_prompt = (kt, vt, lft)
            c, = _cumsum_time(lft, j)
            mp = _attention_prompt(q, kt, vt, c.reshape(bp, nh // 2, 2, tp), slot=j, tq=tq, hd=hd,
                                   pairs=ATTN_PAIRS)
            mp = mp.reshape(np_, d)

            q, k, v, lf = _fox_proj(xs, wq, wk, wv, wf, bf, nh=nh, q_scale=q_scale, tm=ns)
            cc, cn = _cumsum_time(cache_lft, j, jnp.swapaxes(lf.reshape(bs, ts, nh), 1, 2))
            ms = _attention_sample(q.reshape(bs, ts, d), cache_kt, cache_vt,
                                   k.reshape(bs, ts, d), v.reshape(bs, ts, d),
                                   cc.reshape(bs, nh // 2, 2, past), cn.reshape(bs, nh // 2, 2, ts),
                                   slot=j, hd=hd)
            ms = ms.reshape(ns, d)
            outs["fk_s"].append(k.reshape(bs, ts, nh, hd))
            outs["fv_s"].append(v.reshape(bs, ts, nh, hd))
            outs["flf_s"].append(lf.reshape(bs, ts, nh))
        elif kind == 1:
            w_in = w_hgrn_in[j].astype(BF16)
            wo = w_hgrn_out_b
            ng = hgrn_norm_g[j].reshape(1, hg_dv)
            s0 = jnp.zeros((bp, hg_heads, hg_dv, hg_dk), F32)
            mp, st = _hgrn(xp.reshape(bp, tp, d), w_in, hgrn_lb, ng, s0, layer=i, tt=tm_p)
            mp = mp.reshape(np_, d)
            outs["hs_p"].append(jnp.swapaxes(st, 2, 3))
            ms, st = _hgrn(xs.reshape(bs, ts, d), w_in, hgrn_lb, ng,
                           jnp.swapaxes(state_hgrn[j], 2, 3), layer=i, tt=ts)
            ms = ms.reshape(ns, d)
            outs["hs_s"].append(jnp.swapaxes(st, 2, 3))
        else:
            wp = w_pool[j].astype(BF16)
            sc = pool_scale[j].reshape(1, d)
            xp3 = xp.reshape(bp, tp, d)
            xs3 = xs.reshape(bs, ts, d)
            hist_p = jnp.zeros((bp, hpad, d), F32)
            hist_s = jnp.pad(cache_pool[j], ((0, 0), (hpad - hist, 0), (0, 0)))
            mp = _pool(xp3, hist_p, wp, sc, pos0=0, tm=tm_p).reshape(np_, d)
            ms = _pool(xs3, hist_s, wp, sc, pos0=past, tm=ts).reshape(ns, d)
            outs["pl_p"].append(xp3[:, tp - hist:])
            outs["pl_s"].append(jnp.concatenate([cache_pool[j], xs3], axis=1)[:, ts:])
        xp = _post(xp, mp, wo, j, ln_g, ln_b, wi, w2, i, alpha=alpha, tm=2 * POST_SUB)
        xs = _post(xs, ms, wo, j, ln_g, ln_b, wi, w2, i, alpha=alpha, tm=ns)

    stack = lambda name: jnp.stack(outs[name])
    kt, vt, lft = fox_prompt
    time_major = lambda a: jnp.transpose(a.reshape(n_fox, bp, nh, hd, tp), (0, 1, 4, 2, 3))
    return (xp.reshape(bp, tp, d), xs.reshape(bs, ts, d),
            time_major(kt), time_major(vt), jnp.swapaxes(lft, 2, 3), stack("hs_p"), stack("pl_p"),
            stack("fk_s"), stack("fv_s"), stack("flf_s"), stack("hs_s"), stack("pl_s"))
```

```python
import functools

import jax
import jax.numpy as jnp
from jax import lax
from jax.experimental import pallas as pl
from jax.experimental.pallas import tpu as pltpu

F32 = jnp.float32
BF16 = jnp.bfloat16

N_MIXERS = 3
POOL_WINDOWS = (2, 4, 8, 16)
LN_EPS = 1e-5
RMS_EPS = 1e-6
LOG2E = 1.4426950408889634

LANES = 128
VMEM_LIMIT_BYTES = 56 * 1024 * 1024
HGRN_CHUNK = 128
HGRN_SUB = 16
HGRN_SAFE_EXPONENT = 80.0
FFN_CHUNK = 256
POST_TILE = 1024
POST_SUB = 256
ATTN_PAIRS = 2


def _params(n_axes, vmem=VMEM_LIMIT_BYTES):
    return pltpu.CompilerParams(dimension_semantics=("arbitrary",) * n_axes,
                                vmem_limit_bytes=vmem)


def _const_spec(shape):
    nd = len(shape)
    return pl.BlockSpec(shape, lambda *_: (0,) * nd, pipeline_mode=pl.Buffered(1))


def _dot(a, b):
    return jnp.dot(a, b, preferred_element_type=F32)


def _layer_norm(z, g, b):
    mu = jnp.mean(z, axis=-1, keepdims=True)
    zc = z - mu
    var = jnp.mean(zc * zc, axis=-1, keepdims=True)
    return zc * lax.rsqrt(var + LN_EPS) * g + b


def _silu(a):
    return a * jax.nn.sigmoid(a)


def _post_body(*refs, alpha, hidden, has_wo, sub):
    if has_wo:
        x_ref, m_ref, wo_ref, g_ref, b_ref, wi_ref, w2_ref, o_ref, h_ref, x1_ref = refs
    else:
        x_ref, m_ref, g_ref, b_ref, wi_ref, w2_ref, o_ref, h_ref, x1_ref = refs
    groups = [slice(r, r + sub) for r in range(0, x_ref.shape[0], sub)]
    for rs in groups:
        mix = _dot(m_ref[rs, :], wo_ref[0]) if has_wo else m_ref[rs, :]
        x1_ref[rs, :] = _layer_norm(alpha * x_ref[rs, :] + mix, g_ref[0, 0:1, :], b_ref[0, 0:1, :])
    _ffn_and_norm(x1_ref, g_ref, b_ref, wi_ref, w2_ref, o_ref, h_ref, groups, alpha=alpha, hidden=hidden)


def _ffn_and_norm(x1_ref, g_ref, b_ref, wi_ref, w2_ref, o_ref, h_ref, groups, *, alpha, hidden):
    for rs in groups:
        xb = x1_ref[rs, :].astype(BF16)
        for c in range(hidden // FFN_CHUNK):
            lo = c * FFN_CHUNK
            a = _dot(xb, wi_ref[0, :, lo:lo + FFN_CHUNK])
            u = _dot(xb, wi_ref[0, :, hidden + lo:hidden + lo + FFN_CHUNK])
            h_ref[rs, lo:lo + FFN_CHUNK] = (_silu(a) * u).astype(BF16)
    for rs in groups:
        f = _dot(h_ref[rs, :], w2_ref[0])
        o_ref[rs, :] = _layer_norm(alpha * x1_ref[rs, :] + f, g_ref[0, 1:2, :], b_ref[0, 1:2, :])


def _post_pool_body(x_ref, halo_ref, wp_ref, sc_ref, g_ref, b_ref, wi_ref, w2_ref, o_ref, h_ref, x1_ref,
                    *, alpha, hidden, sub, tiles_per_seq):
    tm, d = x_ref.shape
    hp = halo_ref.shape[0]
    ti = pl.program_id(0) % tiles_per_seq
    gw = d // len(POOL_WINDOWS)
    pos = ti * tm + lax.broadcasted_iota(jnp.int32, (tm, 1), 0)
    hist = jnp.where(ti == 0, 0.0, halo_ref[...])
    for gi, w in enumerate(POOL_WINDOWS):
        ls = slice(gi * gw, (gi + 1) * gw)
        xg = x_ref[:, ls]
        acc = xg
        head = jnp.concatenate([hist[:, ls], xg[0:hp]], axis=0)
        sh = 1
        while sh < w:
            acc = acc + pltpu.roll(acc, sh, axis=0)
            head = head + pltpu.roll(head, sh, axis=0)
            sh *= 2
        ws = jnp.concatenate([head[hp:], acc[hp:]], axis=0)
        z = ws / jnp.minimum(pos + 1, w).astype(F32) - xg
        x1_ref[:, ls] = _dot(z.astype(BF16), wp_ref[gi]) * sc_ref[:, ls]
    groups = [slice(r, r + sub) for r in range(0, tm, sub)]
    for rs in groups:
        x1_ref[rs, :] = _layer_norm(alpha * x_ref[rs, :] + x1_ref[rs, :], g_ref[0, 0:1, :], b_ref[0, 0:1, :])
    _ffn_and_norm(x1_ref, g_ref, b_ref, wi_ref, w2_ref, o_ref, h_ref, groups, alpha=alpha, hidden=hidden)


def _layer_spec(shape, layer):
    rest = (0,) * (len(shape) - 1)
    return pl.BlockSpec((1,) + tuple(shape[1:]), lambda *_: (layer,) + rest, pipeline_mode=pl.Buffered(1))


def _post(x, m, wo, wo_layer, ln_g, ln_b, wi, w2, layer, *, alpha, tm):
    n, d = x.shape
    hidden = w2.shape[1]
    row = pl.BlockSpec((tm, d), lambda i: (i, 0))
    has_wo = wo is not None
    args = [x, m] + ([wo] if has_wo else []) + [ln_g, ln_b, wi, w2]
    specs = [row, row] + ([_layer_spec(wo.shape, wo_layer)] if has_wo else []) + [
        _layer_spec(ln_g.shape, layer), _layer_spec(ln_b.shape, layer),
        _layer_spec(wi.shape, layer), _layer_spec(w2.shape, layer)]
    return pl.pallas_call(
        functools.partial(_post_body, alpha=alpha, hidden=hidden, has_wo=has_wo, sub=min(tm, POST_SUB)),
        grid=(n // tm,),
        in_specs=specs,
        out_specs=row,
        out_shape=jax.ShapeDtypeStruct((n, d), F32),
        scratch_shapes=[pltpu.VMEM((tm, hidden), BF16), pltpu.VMEM((tm, d), F32)],
        compiler_params=_params(1),
        name="post_ffn",
    )(*args)


def _post_pool(x, seq_len, wp, scale, ln_g, ln_b, wi, w2, layer, *, alpha, tm, hpad):
    n, d = x.shape
    assert seq_len % tm == 0, "a tile must not straddle two sequences"
    hidden = w2.shape[1]
    row = pl.BlockSpec((tm, d), lambda i: (i, 0))
    per = tm // hpad
    halo = pl.BlockSpec((hpad, d), lambda i: (jnp.maximum(i * per - 1, 0), 0))
    return pl.pallas_call(
        functools.partial(_post_pool_body, alpha=alpha, hidden=hidden, sub=min(tm, POST_SUB),
                          tiles_per_seq=seq_len // tm),
        grid=(n // tm,),
        in_specs=[row, halo, _const_spec(wp.shape), _const_spec(scale.shape),
                  _layer_spec(ln_g.shape, layer), _layer_spec(ln_b.shape, layer),
                  _layer_spec(wi.shape, layer), _layer_spec(w2.shape, layer)],
        out_specs=row,
        out_shape=jax.ShapeDtypeStruct((n, d), F32),
        scratch_shapes=[pltpu.VMEM((tm, hidden), BF16), pltpu.VMEM((tm, d), F32)],
        compiler_params=_params(1),
        name="post_pool_ffn",
    )(x, x, wp, scale, ln_g, ln_b, wi, w2)


def _log_sigmoid(z):
    return jnp.minimum(z, 0.0) - jnp.log1p(jnp.exp(-jnp.abs(z)))


def _nt(a, b):
    return lax.dot_general(a, b, (((1,), (1,)), ((), ())), preferred_element_type=F32)


def _tn(a, b):
    return lax.dot_general(a, b, (((0,), (0,)), ((), ())), preferred_element_type=F32)


def _fox_proj_t_body(*refs, scale, nh, aliased, own):
    x_ref, wq_ref, wkvf_ref, bf_ref = refs[:4]
    q_ref, kt_ref, vt_ref, lft_ref = refs[4 + (3 if aliased else 0):]
    d = x_ref.shape[2]
    xb = x_ref[0].astype(BF16)
    q_ref[0] = (_dot(xb, wq_ref[...]) * scale).astype(BF16)
    kvf = _nt(wkvf_ref[...], xb)
    kt_ref[own, 0] = kvf[0:d]
    vt_ref[own, 0] = kvf[d:2 * d]
    lft_ref[own, 0] = _log_sigmoid(kvf[2 * d:2 * d + nh] + bf_ref[...])
    for ref in (kt_ref, vt_ref, lft_ref):
        for s in range(ref.shape[0]):
            if s != own:
                ref[s, 0] = jnp.zeros(ref.shape[2:], F32)


def _fox_proj_t(x, wq, wkvf, bf_col, prev, *, layer_slot, n_slots, q_scale, tm):
    bsz, t, d = x.shape
    nh = bf_col.shape[0]
    aliased = prev is not None
    xspec = pl.BlockSpec((1, tm, d), lambda i, j: (i, j, 0))
    if aliased:
        kspec = pl.BlockSpec((1, 1, d, tm), lambda i, j: (layer_slot, i, 0, j))
        lspec = pl.BlockSpec((1, 1, nh, tm), lambda i, j: (layer_slot, i, 0, j))
    else:
        kspec = pl.BlockSpec((n_slots, 1, d, tm), lambda i, j: (0, i, 0, j))
        lspec = pl.BlockSpec((n_slots, 1, nh, tm), lambda i, j: (0, i, 0, j))
    anyspec = pl.BlockSpec(memory_space=pl.ANY)
    n_in = 4
    return pl.pallas_call(
        functools.partial(_fox_proj_t_body, scale=q_scale, nh=nh, aliased=aliased,
                          own=0 if aliased else layer_slot),
        grid=(bsz, t // tm),
        in_specs=[xspec, _const_spec(wq.shape), _const_spec(wkvf.shape), _const_spec(bf_col.shape)]
                 + ([anyspec] * 3 if aliased else []),
        out_specs=[xspec, kspec, kspec, lspec],
        out_shape=[jax.ShapeDtypeStruct((bsz, t, d), BF16),
                   jax.ShapeDtypeStruct((n_slots, bsz, d, t), F32),
                   jax.ShapeDtypeStruct((n_slots, bsz, d, t), F32),
                   jax.ShapeDtypeStruct((n_slots, bsz, nh, t), F32)],
        input_output_aliases={n_in: 1, n_in + 1: 2, n_in + 2: 3} if aliased else {},
        compiler_params=_params(2),
        name="fox_proj_prompt",
    )(x, wq, wkvf, bf_col, *(prev if aliased else ()))


def _fox_proj_body(x_ref, wq_ref, wk_ref, wv_ref, wf_ref, bf_ref, q_ref, k_ref, v_ref, lf_ref,
                   *, scale, nh):
    xb = x_ref[...].astype(BF16)
    q_ref[...] = (_dot(xb, wq_ref[...]) * scale).astype(BF16)
    k_ref[...] = _dot(xb, wk_ref[...])
    v_ref[...] = _dot(xb, wv_ref[...])
    z = _dot(xb, wf_ref[...]) + bf_ref[...]
    lf_ref[...] = _log_sigmoid(z)[:, :nh]


def _fox_proj(x, wq, wk, wv, wf, bf, *, nh, q_scale, tm):
    n, d = x.shape
    row = pl.BlockSpec((tm, d), lambda i: (i, 0))
    return pl.pallas_call(
        functools.partial(_fox_proj_body, scale=q_scale, nh=nh),
        grid=(n // tm,),
        in_specs=[row, _const_spec(wq.shape), _const_spec(wk.shape), _const_spec(wv.shape),
                  _const_spec(wf.shape), _const_spec(bf.shape)],
        out_specs=[row, row, row, pl.BlockSpec((tm, nh), lambda i: (i, 0))],
        out_shape=[jax.ShapeDtypeStruct((n, d), BF16), jax.ShapeDtypeStruct((n, d), F32),
                   jax.ShapeDtypeStruct((n, d), F32), jax.ShapeDtypeStruct((n, nh), F32)],
        compiler_params=_params(1),
        name="fox_proj_sample",
    )(x, wq, wk, wv, wf, bf)


def _split3(x):
    hi = x.astype(BF16)
    r = x - hi.astype(F32)
    mid = r.astype(BF16)
    lo = (r - mid.astype(F32)).astype(BF16)
    return hi, mid, lo


def _cumsum_body(*refs, has_tail):
    if has_tail:
        lf_ref, tail_ref, c_ref, ctail_ref = refs
    else:
        lf_ref, c_ref = refs
    nh, t = lf_ref.shape[2], lf_ref.shape[3]

    def upper(n):
        r = lax.broadcasted_iota(jnp.int32, (n, n), 0)
        c = lax.broadcasted_iota(jnp.int32, (n, n), 1)
        return (r <= c).astype(BF16)

    def scan(block, tri):
        return _dot(jnp.concatenate(list(_split3(block)), axis=1), jnp.concatenate([tri] * 3, axis=0))

    nblk = t // LANES
    rows = nblk * nh
    stacked = jnp.concatenate([lf_ref[0, 0, :, j * LANES:(j + 1) * LANES] for j in range(nblk)], axis=0)
    local = scan(stacked, upper(LANES))
    r = lax.broadcasted_iota(jnp.int32, (rows, rows), 0)
    c = lax.broadcasted_iota(jnp.int32, (rows, rows), 1)
    earlier = ((c < r) & ((r - c) % nh == 0)).astype(BF16)
    before = _dot(jnp.concatenate([earlier] * 3, axis=1), jnp.concatenate(list(_split3(stacked)), axis=0))
    total = local + jnp.sum(before, axis=-1, keepdims=True)
    for j in range(nblk):
        c_ref[0, :, j * LANES:(j + 1) * LANES] = total[j * nh:(j + 1) * nh]
    if has_tail:
        ctail_ref[0] = scan(tail_ref[0], upper(tail_ref.shape[2])) + total[rows - nh:rows, LANES - 1:LANES]


def _cumsum_time(lf_t, slot, tail=None):
    _, b, nh, t = lf_t.shape
    has_tail = tail is not None
    in_specs = [pl.BlockSpec((1, 1, nh, t), lambda i: (slot, i, 0, 0))]
    out_specs = [pl.BlockSpec((1, nh, t), lambda i: (i, 0, 0))]
    out_shape = [jax.ShapeDtypeStruct((b, nh, t), F32)]
    if has_tail:
        tspec = pl.BlockSpec((1,) + tail.shape[1:], lambda i: (i, 0, 0))
        in_specs.append(tspec)
        out_specs.append(tspec)
        out_shape.append(jax.ShapeDtypeStruct(tail.shape, F32))
    return pl.pallas_call(
        functools.partial(_cumsum_body, has_tail=has_tail), grid=(b,),
        in_specs=in_specs, out_specs=out_specs, out_shape=out_shape,
        compiler_params=_params(1), name="fox_cumsum",
    )(lf_t, *((tail,) if has_tail else ()))


def _head_select(hd):
    lane = lax.broadcasted_iota(jnp.int32, (1, 2 * hd), 1)
    return lane < hd


def _one_head(q2, first, h):
    keep = first if h == 0 else jnp.logical_not(first)
    return jnp.where(keep, q2, jnp.zeros_like(q2))


def _split3_f32(x):
    hi = x.astype(BF16).astype(F32)
    r = x - hi
    mid = r.astype(BF16).astype(F32)
    return hi, mid, r - mid


def _attn_prompt_body(q_ref, kt_ref, vt_ref, c_ref, o_ref, kb, vb, *, tq, hd):
    t = q_ref.shape[1]
    w = 2 * hd
    pairs = q_ref.shape[2] // w
    first = _head_select(hd)
    lane = lax.broadcasted_iota(jnp.int32, (1, w), 1)
    cls = []
    for pp in range(pairs):
        cl = c_ref[0, pp] * LOG2E
        cls.append(cl)
        kb[pp, 0:w, :] = kt_ref[0, 0, pp * w:(pp + 1) * w, :].astype(BF16)
        kb[pp, w:2 * w, :] = jnp.concatenate(
            list(_split3_f32(-cl)) + [jnp.ones((3, t), F32), jnp.zeros((w - 9, t), F32)], axis=0).astype(BF16)
        vb[pp] = vt_ref[0, 0, pp * w:(pp + 1) * w, :].astype(BF16)
    col = lax.broadcasted_iota(jnp.int32, (2 * tq, tq), 1)
    row = lax.broadcasted_iota(jnp.int32, (2 * tq, tq), 0)
    causal = col <= jnp.where(row >= tq, row - tq, row)

    def scores(pp, q0):
        k1 = q0 + tq
        q2 = q_ref[0, q0:k1, pp * w:(pp + 1) * w]
        qa = []
        for h in range(2):
            c_hi, c_mid, c_lo = _split3_f32(cls[pp][h:h + 1, q0:q0 + 1])
            extra = jnp.where((lane == h) | (lane == 2 + h) | (lane == 4 + h), 1.0, 0.0)
            extra = extra + jnp.where(lane == 6, c_hi, 0.0) + jnp.where(lane == 7, c_mid, 0.0) \
                + jnp.where(lane == 8, c_lo, 0.0)
            qa.append(jnp.concatenate([_one_head(q2, first, h),
                                       jnp.broadcast_to(extra.astype(BF16), (tq, w))], axis=1))
        z = _dot(jnp.concatenate(qa, axis=0), kb[pp, :, 0:k1])
        zd = jnp.where(causal, z[:, q0:], -jnp.inf)
        return jnp.concatenate([z[:, :q0], zd], axis=1) if q0 > 0 else zd

    def finish(pp, q0, z):
        k1 = q0 + tq
        m = jnp.max(z, axis=-1, keepdims=True)
        p = jnp.exp2(z - m)
        l = jnp.sum(p, axis=-1, keepdims=True)
        o = _nt(p.astype(BF16), vb[pp, :, 0:k1]) / l
        o_ref[0, q0:k1, pp * w:(pp + 1) * w] = jnp.where(first, o[:tq], o[tq:]).astype(BF16)

    starts = list(range(0, t, tq))
    order = [(pp, q0) for q0 in starts[1::2] + starts[::2][::-1] for pp in range(pairs)]
    z = scores(*order[0])
    for n, item in enumerate(order):
        z_next = scores(*order[n + 1]) if n + 1 < len(order) else None
        finish(*item, z)
        z = z_next


def _attention_prompt(q, kt, vt, c, *, slot, tq, hd, pairs):
    b, t, d = q.shape
    w = 2 * hd * pairs
    qspec = pl.BlockSpec((1, t, w), lambda i, j: (i, 0, j))
    kspec = pl.BlockSpec((1, 1, w, t), lambda i, j: (slot, i, j, 0))
    cspec = pl.BlockSpec((1, pairs, 2, t), lambda i, j: (i, j, 0, 0))
    return pl.pallas_call(
        functools.partial(_attn_prompt_body, tq=tq, hd=hd),
        grid=(b, d // w),
        in_specs=[qspec, kspec, kspec, cspec],
        out_specs=qspec,
        out_shape=jax.ShapeDtypeStruct((b, t, d), BF16),
        scratch_shapes=[pltpu.VMEM((pairs, 4 * hd, t), BF16), pltpu.VMEM((pairs, 2 * hd, t), BF16)],
        compiler_params=_params(2),
        name="fox_attention_prompt",
    )(q, kt, vt, c)


def _attn_sample_body(q_ref, kc_ref, vc_ref, kn_ref, vn_ref, cc_ref, cn_ref, o_ref, *, hd):
    ts = q_ref.shape[1]
    first = _head_select(hd)
    q2 = q_ref[0]
    kc = kc_ref[0, 0].astype(BF16)
    vc = vc_ref[0, 0].astype(BF16)
    kn = kn_ref[0].astype(BF16)
    vn = vn_ref[0].astype(BF16)
    row = lax.broadcasted_iota(jnp.int32, (2 * ts, ts), 0)
    causal = lax.broadcasted_iota(jnp.int32, (2 * ts, ts), 1) <= jnp.where(row >= ts, row - ts, row)
    qs = jnp.concatenate([_one_head(q2, first, 0), _one_head(q2, first, 1)], axis=0)

    def bias(c_ref):
        c = c_ref[0, 0]
        c0 = cn_ref[0, 0][:, 0:1]
        b = (c0 - c) * LOG2E
        return jnp.concatenate([jnp.broadcast_to(b[0:1], (ts, b.shape[1])),
                                jnp.broadcast_to(b[1:2], (ts, b.shape[1]))], axis=0)

    zc = _dot(qs, kc) + bias(cc_ref)
    zn = jnp.where(causal, _nt(qs, kn) + bias(cn_ref), -jnp.inf)
    m = jnp.maximum(jnp.max(zc, axis=-1, keepdims=True), jnp.max(zn, axis=-1, keepdims=True))
    pc = jnp.exp2(zc - m)
    pn = jnp.exp2(zn - m)
    l = jnp.sum(pc, axis=-1, keepdims=True) + jnp.sum(pn, axis=-1, keepdims=True)
    o = (_nt(pc.astype(BF16), vc) + _dot(pn.astype(BF16), vn)) / l
    o_ref[0] = jnp.where(first, o[:ts], o[ts:]).astype(BF16)


def _attention_sample(q, kc, vc, kn, vn, cc, cn, *, slot, hd):
    b, ts, d = q.shape
    past = kc.shape[3]
    nhp = d // (2 * hd)
    qspec = pl.BlockSpec((1, ts, 2 * hd), lambda i, j: (i, 0, j))
    kspec = pl.BlockSpec((1, 1, 2 * hd, past), lambda i, j: (slot, i, j, 0))
    return pl.pallas_call(
        functools.partial(_attn_sample_body, hd=hd),
        grid=(b, nhp),
        in_specs=[qspec, kspec, kspec, qspec, qspec,
                  pl.BlockSpec((1, 1, 2, past), lambda i, j: (i, j, 0, 0)),
                  pl.BlockSpec((1, 1, 2, ts), lambda i, j: (i, j, 0, 0))],
        out_specs=qspec,
        out_shape=jax.ShapeDtypeStruct((b, ts, d), BF16),
        compiler_params=_params(2),
        name="fox_attention_sample",
    )(q, kc, vc, kn, vn, cc, cn)


def _hgrn_body(x_ref, w_ref, lb_ref, ng_ref, s0_ref, o_ref, sT_ref,
               state, state0, q_s, k_s, b_s, v_s, g_s, *, layer, nh, dk, chunk):
    tt, d = x_ref.shape[1], x_ref.shape[2]
    ti = pl.program_id(1)

    @pl.when(ti == 0)
    def _():
        state[...] = s0_ref[0]

    xb = x_ref[0].astype(BF16)
    lbp = lb_ref[...]
    e = jnp.exp(lbp - jnp.max(lbp, axis=0, keepdims=True))
    lb = jnp.sum(e[1:layer + 1, :], axis=0, keepdims=True) / jnp.sum(e, axis=0, keepdims=True)
    fx = _dot(xb, w_ref[:, d:2 * d])
    q_s[...] = _silu(_dot(xb, w_ref[:, 0:d]))
    v = _dot(xb, w_ref[:, 2 * d:3 * d])
    v_s[...] = v
    g_s[...] = _silu(_dot(xb, w_ref[:, 3 * d:4 * d]))
    k_s[...] = (1.0 - lb) * jax.nn.sigmoid(-fx)
    g = jnp.log(lb + (1.0 - lb) * jax.nn.sigmoid(fx))
    tri = (lax.broadcasted_iota(jnp.int32, (chunk, chunk), 0)
           >= lax.broadcasted_iota(jnp.int32, (chunk, chunk), 1)).astype(BF16)
    tri3 = jnp.concatenate([tri, tri, tri], axis=1)
    mid = chunk // 2 - 1
    ng = ng_ref[...]
    vb = v.astype(BF16)
    state0[...] = state[...]
    causal = (lax.broadcasted_iota(jnp.int32, (chunk, chunk), 0)
              >= lax.broadcasted_iota(jnp.int32, (chunk, chunk), 1))

    def finish(h, r0, rows, o, st, bl, upd):
        ls = slice(h * dk, (h + 1) * dk)
        state[h] = jnp.exp(bl) * st + upd
        ms = jnp.mean(o * o, axis=-1, keepdims=True)
        on = o * lax.rsqrt(ms + RMS_EPS) * ng * g_s[pl.ds(r0, rows), ls]
        o_ref[0, pl.ds(r0, rows), ls] = on.astype(BF16)

    spread = jnp.zeros((1, 1), F32)
    for c in range(tt // chunk):
        r0 = c * chunk
        rows = slice(r0, r0 + chunk)
        bc = _dot(tri3, jnp.concatenate(list(_split3(g[rows])), axis=0))
        b_s[rows, :] = bc
        bm = bc[mid:mid + 1, :]
        bl = bc[chunk - 1:chunk, :]
        dd = bc - bm
        spread = jnp.maximum(spread, jnp.max(jnp.abs(dd), keepdims=True))
        qt = (q_s[rows, :] * jnp.exp(dd)).astype(BF16)
        kh = k_s[rows, :] * jnp.exp(-dd)
        kt = (kh * jnp.exp(bl - bm)).astype(BF16)
        kh = kh.astype(BF16)
        ebm = jnp.exp(bm)
        for h in range(nh):
            ls = slice(h * dk, (h + 1) * dk)
            st = state[h]
            dv = st.shape[0]
            rhs = jnp.concatenate([(st * ebm[:, ls]).astype(BF16), kh[:, ls]], axis=0)
            both = _nt(qt[:, ls], rhs)
            sc = jnp.where(causal, both[:, dv:], 0.0).astype(BF16)
            o = _dot(sc, vb[rows, ls]) + both[:, :dv]
            finish(h, r0, chunk, o, st, bl[:, ls], _tn(vb[rows, ls], kt[:, ls]))
    factorable = spread[0, 0] <= HGRN_SAFE_EXPONENT

    @pl.when(jnp.logical_not(factorable))
    def _():
        state[...] = state0[...]
        sub = min(HGRN_SUB, chunk)
        row = lax.broadcasted_iota(jnp.int32, (sub, 1), 0)

        def step(i, carry):
            r0 = pl.multiple_of(i * sub, sub)
            inside = (r0 % chunk) != 0
            prev = pl.multiple_of(jnp.maximum(r0 - 8, 0), 8)
            for h in range(nh):
                ls = slice(h * dk, (h + 1) * dk)
                qs = q_s[pl.ds(r0, sub), ls]
                ks = k_s[pl.ds(r0, sub), ls]
                vs = v_s[pl.ds(r0, sub), ls]
                base = jnp.where(inside, b_s[pl.ds(prev, 8), ls][7:8, :], 0.0)
                bs = b_s[pl.ds(r0, sub), ls] - base
                bl = bs[sub - 1:sub, :]
                od = jnp.zeros((sub, dk), F32)
                for s in range(sub):
                    dec = jnp.exp(jnp.where(row >= s, bs - bs[s:s + 1, :], -jnp.inf))
                    r = jnp.sum(qs * ks[s:s + 1, :] * dec, axis=-1, keepdims=True)
                    od = od + r * vs[s:s + 1, :]
                st = state[h]
                qt = (qs * jnp.exp(bs)).astype(BF16)
                kt = (ks * jnp.exp(bl - bs)).astype(BF16)
                o = od + _nt(qt, st.astype(BF16))
                finish(h, r0, sub, o, st, bl, _tn(vs.astype(BF16), kt))
            return carry

        lax.fori_loop(0, tt // sub, step, 0)

    @pl.when(ti == pl.num_programs(1) - 1)
    def _():
        sT_ref[0] = state[...]


def _hgrn(x, w, lb, ng, s0_t, *, layer, tt):
    bsz, t, d = x.shape
    nh, dv, dk = s0_t.shape[1:]
    chunk = min(HGRN_CHUNK, tt)
    blk = pl.BlockSpec((1, tt, d), lambda i, j: (i, j, 0))
    sspec = pl.BlockSpec((1, nh, dv, dk), lambda i, j: (i, 0, 0, 0))
    return pl.pallas_call(
        functools.partial(_hgrn_body, layer=layer, nh=nh, dk=dk, chunk=chunk),
        grid=(bsz, t // tt),
        in_specs=[blk, _const_spec(w.shape), _const_spec(lb.shape), _const_spec(ng.shape), sspec],
        out_specs=[blk, sspec],
        out_shape=[jax.ShapeDtypeStruct((bsz, t, d), BF16),
                   jax.ShapeDtypeStruct((bsz, nh, dv, dk), F32)],
        scratch_shapes=[pltpu.VMEM((nh, dv, dk), F32)] * 2 + [pltpu.VMEM((tt, d), F32)] * 5,
        compiler_params=_params(2), name="hgrn_mix",
    )(x, w, lb, ng, s0_t)


def _pool_body(x_ref, halo_ref, hist_ref, w_ref, sc_ref, y_ref, ext, *, pos0, hpad):
    tm = x_ref.shape[1]
    d = x_ref.shape[2]
    ti = pl.program_id(1)
    gw = d // len(POOL_WINDOWS)

    @pl.when(ti == 0)
    def _():
        ext[0:hpad, :] = hist_ref[0]

    @pl.when(ti > 0)
    def _():
        ext[0:hpad, :] = halo_ref[0]

    ext[hpad:hpad + tm, :] = x_ref[0]
    pos = pos0 + ti * tm + lax.broadcasted_iota(jnp.int32, (tm, 1), 0)
    ys = []
    for gi, w in enumerate(POOL_WINDOWS):
        ls = slice(gi * gw, (gi + 1) * gw)
        acc = ext[:, ls]
        xg = acc[hpad:, :]
        sh = 1
        while sh < w:
            acc = acc + pltpu.roll(acc, sh, axis=0)
            sh *= 2
        count = jnp.minimum(pos + 1, w).astype(F32)
        z = acc[hpad:, :] / count - xg
        ys.append(_dot(z.astype(BF16), w_ref[gi]))
    y_ref[0] = jnp.concatenate(ys, axis=1) * sc_ref[...]


def _pool(x, hist, w, scale, *, pos0, tm):
    bsz, t, d = x.shape
    hpad = hist.shape[1]
    blk = pl.BlockSpec((1, tm, d), lambda i, j: (i, j, 0))
    per = tm // hpad
    halo = pl.BlockSpec((1, hpad, d), lambda i, j: (i, jnp.maximum(j * per - 1, 0), 0))
    return pl.pallas_call(
        functools.partial(_pool_body, pos0=pos0, hpad=hpad),
        grid=(bsz, t // tm),
        in_specs=[blk, halo, pl.BlockSpec((1, hpad, d), lambda i, j: (i, 0, 0)),
                  pl.BlockSpec(w.shape, lambda i, j: (0, 0, 0)),
                  pl.BlockSpec(scale.shape, lambda i, j: (0, 0))],
        out_specs=blk,
        out_shape=jax.ShapeDtypeStruct((bsz, t, d), F32),
        scratch_shapes=[pltpu.VMEM((hpad + tm, d), F32)],
        compiler_params=_params(2), name="pool_mix",
    )(x, x, hist, w, scale)


def kernel(x_prompt, x_sample, cache_fox_k, cache_fox_v, cache_fox_lf, state_hgrn, cache_pool,
           w_fox_in, b_fox_f, w_fox_out, w_hgrn_in, hgrn_lb, hgrn_norm_g, w_hgrn_out,
           w_pool, pool_scale, w_ffn_in, w_ffn_out, ln_g, ln_b):
    bp, tp, d = x_prompt.shape
    bs, ts, _ = x_sample.shape
    depth = w_ffn_in.shape[0]
    nh = b_fox_f.shape[1]
    hd = d // nh
    past = cache_fox_k.shape[2]
    hist = cache_pool.shape[2]
    hg_heads, hg_dk, hg_dv = state_hgrn.shape[2:]
    alpha = (2 * depth) ** 0.25
    np_, ns = bp * tp, bs * ts
    tm_p = 512
    tq = 256
    hpad = 16

    xp = x_prompt.reshape(np_, d)
    xs = x_sample.reshape(ns, d)
    outs = {name: [] for name in ("hs_p", "pl_p", "fk_s", "fv_s", "flf_s", "hs_s", "pl_s")}
    n_fox = w_fox_in.shape[0]
    q_scale = LOG2E * hd ** -0.5
    fox_prompt = None
    cache_kt = jnp.transpose(cache_fox_k, (0, 1, 3, 4, 2)).reshape(n_fox, bs, d, past)
    cache_vt = jnp.transpose(cache_fox_v, (0, 1, 3, 4, 2)).reshape(n_fox, bs, d, past)
    cache_lft = jnp.swapaxes(cache_fox_lf, 2, 3)
    wi = w_ffn_in.astype(BF16)
    w2 = w_ffn_out.astype(BF16)
    w_fox_out_b = w_fox_out.astype(BF16)
    w_hgrn_out_b = w_hgrn_out.astype(BF16)

    for i in range(depth):
        j = i // N_MIXERS
        kind = i % N_MIXERS
        wo = None
        if kind == 0:
            w_in = w_fox_in[j]
            wq = w_in[:, :d].astype(BF16)
            wk = w_in[:, d:2 * d].astype(BF16)
            wv = w_in[:, 2 * d:3 * d].astype(BF16)
            wf = jnp.pad(w_in[:, 3 * d:], ((0, 0), (0, LANES - nh))).astype(BF16)
            wkvf = jnp.swapaxes(jnp.pad(w_in[:, d:], ((0, 0), (0, LANES - nh))), 0, 1).astype(BF16)
            bf = jnp.pad(b_fox_f[j], (0, LANES - nh)).reshape(1, LANES)
            wo = w_fox_out_b

            q, kt, vt, lft = _fox_proj_t(xp.reshape(bp, tp, d), wq, wkvf, b_fox_f[j].reshape(nh, 1),
                                         fox_prompt, layer_slot=j, n_slots=n_fox, q_scale=q_scale, tm=tm_p)
            fox_prompt = (kt, vt, lft)
            c, = _cumsum_time(lft, j)
            mp = _attention_prompt(q, kt, vt, c.reshape(bp, nh // 2, 2, tp), slot=j, tq=tq, hd=hd,
                                   pairs=ATTN_PAIRS)
            mp = mp.reshape(np_, d)

            q, k, v, lf = _fox_proj(xs, wq, wk, wv, wf, bf, nh=nh, q_scale=q_scale, tm=ns)
            cc, cn = _cumsum_time(cache_lft, j, jnp.swapaxes(lf.reshape(bs, ts, nh), 1, 2))
            ms = _attention_sample(q.reshape(bs, ts, d), cache_kt, cache_vt,
                                   k.reshape(bs, ts, d), v.reshape(bs, ts, d),
                                   cc.reshape(bs, nh // 2, 2, past), cn.reshape(bs, nh // 2, 2, ts),
                                   slot=j, hd=hd)
            ms = ms.reshape(ns, d)
            outs["fk_s"].append(k.reshape(bs, ts, nh, hd))
            outs["fv_s"].append(v.reshape(bs, ts, nh, hd))
            outs["flf_s"].append(lf.reshape(bs, ts, nh))
        elif kind == 1:
            w_in = w_hgrn_in[j].astype(BF16)
            wo = w_hgrn_out_b
            ng = hgrn_norm_g[j].reshape(1, hg_dv)
            s0 = jnp.zeros((bp, hg_heads, hg_dv, hg_dk), F32)
            mp, st = _hgrn(xp.reshape(bp, tp, d), w_in, hgrn_lb, ng, s0, layer=i, tt=tm_p)
            mp = mp.reshape(np_, d)
            outs["hs_p"].append(jnp.swapaxes(st, 2, 3))
            ms, st = _hgrn(xs.reshape(bs, ts, d), w_in, hgrn_lb, ng,
                           jnp.swapaxes(state_hgrn[j], 2, 3), layer=i, tt=ts)
            ms = ms.reshape(ns, d)
            outs["hs_s"].append(jnp.swapaxes(st, 2, 3))
        else:
            wp = w_pool[j].astype(BF16)
            sc = pool_scale[j].reshape(1, d)
            xp3 = xp.reshape(bp, tp, d)
            xs3 = xs.reshape(bs, ts, d)
            hist_s = jnp.pad(cache_pool[j], ((0, 0), (hpad - hist, 0), (0, 0)))
            mp = None
            ms = _pool(xs3, hist_s, wp, sc, pos0=past, tm=ts).reshape(ns, d)
            outs["pl_p"].append(xp3[:, tp - hist:])
            outs["pl_s"].append(jnp.concatenate([cache_pool[j], xs3], axis=1)[:, ts:])
        if mp is None:
            xp = _post_pool(xp, tp, wp, sc, ln_g, ln_b, wi, w2, i, alpha=alpha, tm=POST_TILE, hpad=hpad)
        else:
            xp = _post(xp, mp, wo, j, ln_g, ln_b, wi, w2, i, alpha=alpha, tm=POST_TILE)
        xs = _post(xs, ms, wo, j, ln_g, ln_b, wi, w2, i, alpha=alpha, tm=ns)

    stack = lambda name: jnp.stack(outs[name])
    kt, vt, lft = fox_prompt
    time_major = lambda a: jnp.transpose(a.reshape(n_fox, bp, nh, hd, tp), (0, 1, 4, 2, 3))
    return (xp.reshape(bp, tp, d), xs.reshape(bs, ts, d),
            time_major(kt), time_major(vt), jnp.swapaxes(lft, 2, 3), stack("hs_p"), stack("pl_p"),
            stack("fk_s"), stack("fv_s"), stack("flf_s"), stack("hs_s"), stack("pl_s"))
```

```python
import functools

import jax
import jax.numpy as jnp
from jax import lax
from jax.experimental import pallas as pl
from jax.experimental.pallas import tpu as pltpu

F32 = jnp.float32
BF16 = jnp.bfloat16

N_MIXERS = 3
POOL_WINDOWS = (2, 4, 8, 16)
LN_EPS = 1e-5
RMS_EPS = 1e-6
LOG2E = 1.4426950408889634

LANES = 128
VMEM_LIMIT_BYTES = 56 * 1024 * 1024
HGRN_CHUNK = 128
HGRN_SUB = 16
HGRN_SAFE_EXPONENT = 80.0
FFN_CHUNK = 256
POST_TILE = 1024
POST_SUB = 256
ATTN_PAIRS = 2
SAMPLE_ATTN_PAIRS = 4


def _params(n_axes, vmem=VMEM_LIMIT_BYTES):
    return pltpu.CompilerParams(dimension_semantics=("arbitrary",) * n_axes,
                                vmem_limit_bytes=vmem)


def _const_spec(shape):
    nd = len(shape)
    return pl.BlockSpec(shape, lambda *_: (0,) * nd, pipeline_mode=pl.Buffered(1))


def _dot(a, b):
    return jnp.dot(a, b, preferred_element_type=F32)


def _layer_norm(z, g, b):
    mu = jnp.mean(z, axis=-1, keepdims=True)
    zc = z - mu
    var = jnp.mean(zc * zc, axis=-1, keepdims=True)
    return zc * lax.rsqrt(var + LN_EPS) * g + b


def _silu(a):
    return a * jax.nn.sigmoid(a)


def _post_body(*refs, alpha, hidden, has_wo, sub):
    if has_wo:
        x_ref, m_ref, wo_ref, g_ref, b_ref, wi_ref, w2_ref, o_ref, h_ref, x1_ref = refs
    else:
        x_ref, m_ref, g_ref, b_ref, wi_ref, w2_ref, o_ref, h_ref, x1_ref = refs
    groups = [slice(r, r + sub) for r in range(0, x_ref.shape[0], sub)]
    for rs in groups:
        mix = _dot(m_ref[rs, :], wo_ref[0]) if has_wo else m_ref[rs, :]
        x1_ref[rs, :] = _layer_norm(alpha * x_ref[rs, :] + mix, g_ref[0, 0:1, :], b_ref[0, 0:1, :])
    _ffn_and_norm(x1_ref, g_ref, b_ref, wi_ref, w2_ref, o_ref, h_ref, groups, alpha=alpha, hidden=hidden)


def _ffn_and_norm(x1_ref, g_ref, b_ref, wi_ref, w2_ref, o_ref, h_ref, groups, *, alpha, hidden):
    def expand(rs):
        xb = x1_ref[rs, :].astype(BF16)
        for c in range(hidden // FFN_CHUNK):
            lo = c * FFN_CHUNK
            a = _dot(xb, wi_ref[0, :, lo:lo + FFN_CHUNK])
            u = _dot(xb, wi_ref[0, :, hidden + lo:hidden + lo + FFN_CHUNK])
            h_ref[rs, lo:lo + FFN_CHUNK] = (_silu(a) * u).astype(BF16)

    def contract(rs):
        f = _dot(h_ref[rs, :], w2_ref[0])
        o_ref[rs, :] = _layer_norm(alpha * x1_ref[rs, :] + f, g_ref[0, 1:2, :], b_ref[0, 1:2, :])

    expand(groups[0])
    for n, rs in enumerate(groups):
        if n + 1 < len(groups):
            expand(groups[n + 1])
        contract(rs)


def _post_pool_body(x_ref, halo_ref, wp_ref, sc_ref, g_ref, b_ref, wi_ref, w2_ref, o_ref, h_ref, x1_ref,
                    *, alpha, hidden, sub, tiles_per_seq):
    tm, d = x_ref.shape
    hp = halo_ref.shape[0]
    ti = pl.program_id(0) % tiles_per_seq
    gw = d // len(POOL_WINDOWS)
    groups = [slice(r, r + sub) for r in range(0, tm, sub)]
    for rs in groups:
        prev = jnp.where(ti == 0, 0.0, halo_ref[...]) if rs.start == 0 else x_ref[rs.start - hp:rs.start, :]
        pos = ti * tm + rs.start + lax.broadcasted_iota(jnp.int32, (sub, 1), 0)
        for gi, w in enumerate(POOL_WINDOWS):
            ls = slice(gi * gw, (gi + 1) * gw)
            xg = x_ref[rs, ls]
            acc = xg
            head = jnp.concatenate([prev[:, ls], xg[0:hp]], axis=0)
            sh = 1
            while sh < w:
                acc = acc + pltpu.roll(acc, sh, axis=0)
                head = head + pltpu.roll(head, sh, axis=0)
                sh *= 2
            ws = jnp.concatenate([head[hp:], acc[hp:]], axis=0)
            z = ws / jnp.minimum(pos + 1, w).astype(F32) - xg
            x1_ref[rs, ls] = _dot(z.astype(BF16), wp_ref[gi]) * sc_ref[:, ls]
        x1_ref[rs, :] = _layer_norm(alpha * x_ref[rs, :] + x1_ref[rs, :], g_ref[0, 0:1, :], b_ref[0, 0:1, :])
    _ffn_and_norm(x1_ref, g_ref, b_ref, wi_ref, w2_ref, o_ref, h_ref, groups, alpha=alpha, hidden=hidden)


def _layer_spec(shape, layer):
    rest = (0,) * (len(shape) - 1)
    return pl.BlockSpec((1,) + tuple(shape[1:]), lambda *_: (layer,) + rest, pipeline_mode=pl.Buffered(1))


def _post(x, m, wo, wo_layer, ln_g, ln_b, wi, w2, layer, *, alpha, tm):
    n, d = x.shape
    hidden = w2.shape[1]
    row = pl.BlockSpec((tm, d), lambda i: (i, 0))
    has_wo = wo is not None
    args = [x, m] + ([wo] if has_wo else []) + [ln_g, ln_b, wi, w2]
    specs = [row, row] + ([_layer_spec(wo.shape, wo_layer)] if has_wo else []) + [
        _layer_spec(ln_g.shape, layer), _layer_spec(ln_b.shape, layer),
        _layer_spec(wi.shape, layer), _layer_spec(w2.shape, layer)]
    return pl.pallas_call(
        functools.partial(_post_body, alpha=alpha, hidden=hidden, has_wo=has_wo, sub=min(tm, POST_SUB)),
        grid=(n // tm,),
        in_specs=specs,
        out_specs=row,
        out_shape=jax.ShapeDtypeStruct((n, d), F32),
        scratch_shapes=[pltpu.VMEM((tm, hidden), BF16), pltpu.VMEM((tm, d), F32)],
        compiler_params=_params(1),
        name="post_ffn",
    )(*args)


def _post_pool(x, seq_len, wp, scale, ln_g, ln_b, wi, w2, layer, *, alpha, tm, hpad):
    n, d = x.shape
    assert seq_len % tm == 0, "a tile must not straddle two sequences"
    hidden = w2.shape[1]
    row = pl.BlockSpec((tm, d), lambda i: (i, 0))
    per = tm // hpad
    halo = pl.BlockSpec((hpad, d), lambda i: (jnp.maximum(i * per - 1, 0), 0))
    return pl.pallas_call(
        functools.partial(_post_pool_body, alpha=alpha, hidden=hidden, sub=min(tm, POST_SUB),
                          tiles_per_seq=seq_len // tm),
        grid=(n // tm,),
        in_specs=[row, halo, _const_spec(wp.shape), _const_spec(scale.shape),
                  _layer_spec(ln_g.shape, layer), _layer_spec(ln_b.shape, layer),
                  _layer_spec(wi.shape, layer), _layer_spec(w2.shape, layer)],
        out_specs=row,
        out_shape=jax.ShapeDtypeStruct((n, d), F32),
        scratch_shapes=[pltpu.VMEM((tm, hidden), BF16), pltpu.VMEM((tm, d), F32)],
        compiler_params=_params(1),
        name="post_pool_ffn",
    )(x, x, wp, scale, ln_g, ln_b, wi, w2)


def _log_sigmoid(z):
    return jnp.minimum(z, 0.0) - jnp.log1p(jnp.exp(-jnp.abs(z)))


def _nt(a, b):
    return lax.dot_general(a, b, (((1,), (1,)), ((), ())), preferred_element_type=F32)


def _tn(a, b):
    return lax.dot_general(a, b, (((0,), (0,)), ((), ())), preferred_element_type=F32)


def _fox_proj_t_body(*refs, scale, nh, aliased, own):
    x_ref, wq_ref, wkvf_ref, bf_ref = refs[:4]
    q_ref, kt_ref, vt_ref, lft_ref = refs[4 + (3 if aliased else 0):]
    d = x_ref.shape[2]
    xb = x_ref[0].astype(BF16)
    q_ref[0] = (_dot(xb, wq_ref[...]) * scale).astype(BF16)
    kvf = _nt(wkvf_ref[...], xb)
    kt_ref[own, 0] = kvf[0:d]
    vt_ref[own, 0] = kvf[d:2 * d]
    lft_ref[own, 0] = _log_sigmoid(kvf[2 * d:2 * d + nh] + bf_ref[...])
    for ref in (kt_ref, vt_ref, lft_ref):
        for s in range(ref.shape[0]):
            if s != own:
                ref[s, 0] = jnp.zeros(ref.shape[2:], F32)


def _fox_proj_t(x, wq, wkvf, bf_col, prev, *, layer_slot, n_slots, q_scale, tm):
    bsz, t, d = x.shape
    nh = bf_col.shape[0]
    aliased = prev is not None
    xspec = pl.BlockSpec((1, tm, d), lambda i, j: (i, j, 0))
    if aliased:
        kspec = pl.BlockSpec((1, 1, d, tm), lambda i, j: (layer_slot, i, 0, j))
        lspec = pl.BlockSpec((1, 1, nh, tm), lambda i, j: (layer_slot, i, 0, j))
    else:
        kspec = pl.BlockSpec((n_slots, 1, d, tm), lambda i, j: (0, i, 0, j))
        lspec = pl.BlockSpec((n_slots, 1, nh, tm), lambda i, j: (0, i, 0, j))
    anyspec = pl.BlockSpec(memory_space=pl.ANY)
    n_in = 4
    return pl.pallas_call(
        functools.partial(_fox_proj_t_body, scale=q_scale, nh=nh, aliased=aliased,
                          own=0 if aliased else layer_slot),
        grid=(bsz, t // tm),
        in_specs=[xspec, _const_spec(wq.shape), _const_spec(wkvf.shape), _const_spec(bf_col.shape)]
                 + ([anyspec] * 3 if aliased else []),
        out_specs=[xspec, kspec, kspec, lspec],
        out_shape=[jax.ShapeDtypeStruct((bsz, t, d), BF16),
                   jax.ShapeDtypeStruct((n_slots, bsz, d, t), F32),
                   jax.ShapeDtypeStruct((n_slots, bsz, d, t), F32),
                   jax.ShapeDtypeStruct((n_slots, bsz, nh, t), F32)],
        input_output_aliases={n_in: 1, n_in + 1: 2, n_in + 2: 3} if aliased else {},
        compiler_params=_params(2),
        name="fox_proj_prompt",
    )(x, wq, wkvf, bf_col, *(prev if aliased else ()))


def _fox_proj_body(x_ref, wq_ref, wk_ref, wv_ref, wf_ref, bf_ref, q_ref, k_ref, v_ref, lf_ref,
                   *, scale, nh):
    xb = x_ref[...].astype(BF16)
    q_ref[...] = (_dot(xb, wq_ref[...]) * scale).astype(BF16)
    k_ref[...] = _dot(xb, wk_ref[...])
    v_ref[...] = _dot(xb, wv_ref[...])
    z = _dot(xb, wf_ref[...]) + bf_ref[...]
    lf_ref[...] = _log_sigmoid(z)[:, :nh]


def _fox_proj(x, wq, wk, wv, wf, bf, *, nh, q_scale, tm):
    n, d = x.shape
    row = pl.BlockSpec((tm, d), lambda i: (i, 0))
    return pl.pallas_call(
        functools.partial(_fox_proj_body, scale=q_scale, nh=nh),
        grid=(n // tm,),
        in_specs=[row, _const_spec(wq.shape), _const_spec(wk.shape), _const_spec(wv.shape),
                  _const_spec(wf.shape), _const_spec(bf.shape)],
        out_specs=[row, row, row, pl.BlockSpec((tm, nh), lambda i: (i, 0))],
        out_shape=[jax.ShapeDtypeStruct((n, d), BF16), jax.ShapeDtypeStruct((n, d), F32),
                   jax.ShapeDtypeStruct((n, d), F32), jax.ShapeDtypeStruct((n, nh), F32)],
        compiler_params=_params(1),
        name="fox_proj_sample",
    )(x, wq, wk, wv, wf, bf)


def _split3(x):
    hi = x.astype(BF16)
    r = x - hi.astype(F32)
    mid = r.astype(BF16)
    lo = (r - mid.astype(F32)).astype(BF16)
    return hi, mid, lo


def _cumsum_body(*refs, has_tail):
    if has_tail:
        lf_ref, tail_ref, c_ref, ctail_ref = refs
    else:
        lf_ref, c_ref = refs
    nh, t = lf_ref.shape[2], lf_ref.shape[3]

    def upper(n):
        r = lax.broadcasted_iota(jnp.int32, (n, n), 0)
        c = lax.broadcasted_iota(jnp.int32, (n, n), 1)
        return (r <= c).astype(BF16)

    def scan(block, tri):
        return _dot(jnp.concatenate(list(_split3(block)), axis=1), jnp.concatenate([tri] * 3, axis=0))

    nblk = t // LANES
    rows = nblk * nh
    stacked = jnp.concatenate([lf_ref[0, 0, :, j * LANES:(j + 1) * LANES] for j in range(nblk)], axis=0)
    local = scan(stacked, upper(LANES))
    r = lax.broadcasted_iota(jnp.int32, (rows, rows), 0)
    c = lax.broadcasted_iota(jnp.int32, (rows, rows), 1)
    earlier = ((c < r) & ((r - c) % nh == 0)).astype(BF16)
    before = _dot(jnp.concatenate([earlier] * 3, axis=1), jnp.concatenate(list(_split3(stacked)), axis=0))
    total = local + jnp.sum(before, axis=-1, keepdims=True)
    for j in range(nblk):
        c_ref[0, :, j * LANES:(j + 1) * LANES] = total[j * nh:(j + 1) * nh]
    if has_tail:
        ctail_ref[0] = scan(tail_ref[0], upper(tail_ref.shape[2])) + total[rows - nh:rows, LANES - 1:LANES]


def _cumsum_time(lf_t, slot, tail=None):
    _, b, nh, t = lf_t.shape
    has_tail = tail is not None
    in_specs = [pl.BlockSpec((1, 1, nh, t), lambda i: (slot, i, 0, 0))]
    out_specs = [pl.BlockSpec((1, nh, t), lambda i: (i, 0, 0))]
    out_shape = [jax.ShapeDtypeStruct((b, nh, t), F32)]
    if has_tail:
        tspec = pl.BlockSpec((1,) + tail.shape[1:], lambda i: (i, 0, 0))
        in_specs.append(tspec)
        out_specs.append(tspec)
        out_shape.append(jax.ShapeDtypeStruct(tail.shape, F32))
    return pl.pallas_call(
        functools.partial(_cumsum_body, has_tail=has_tail), grid=(b,),
        in_specs=in_specs, out_specs=out_specs, out_shape=out_shape,
        compiler_params=_params(1), name="fox_cumsum",
    )(lf_t, *((tail,) if has_tail else ()))


def _head_select(hd):
    lane = lax.broadcasted_iota(jnp.int32, (1, 2 * hd), 1)
    return lane < hd


def _one_head(q2, first, h):
    keep = first if h == 0 else jnp.logical_not(first)
    return jnp.where(keep, q2, jnp.zeros_like(q2))


def _split3_f32(x):
    hi = x.astype(BF16).astype(F32)
    r = x - hi
    mid = r.astype(BF16).astype(F32)
    return hi, mid, r - mid


def _attn_prompt_body(q_ref, kt_ref, vt_ref, c_ref, o_ref, kb, vb, *, tq, hd):
    t = q_ref.shape[1]
    w = 2 * hd
    pairs = q_ref.shape[2] // w
    first = _head_select(hd)
    lane = lax.broadcasted_iota(jnp.int32, (1, w), 1)
    cls = []
    for pp in range(pairs):
        cl = c_ref[0, pp] * LOG2E
        cls.append(cl)
        kb[pp, 0:w, :] = kt_ref[0, 0, pp * w:(pp + 1) * w, :].astype(BF16)
        kb[pp, w:2 * w, :] = jnp.concatenate(
            list(_split3_f32(-cl)) + [jnp.ones((3, t), F32), jnp.zeros((w - 9, t), F32)], axis=0).astype(BF16)
        vb[pp] = vt_ref[0, 0, pp * w:(pp + 1) * w, :].astype(BF16)
    col = lax.broadcasted_iota(jnp.int32, (2 * tq, tq), 1)
    row = lax.broadcasted_iota(jnp.int32, (2 * tq, tq), 0)
    causal = col <= jnp.where(row >= tq, row - tq, row)

    def scores(pp, q0):
        k1 = q0 + tq
        q2 = q_ref[0, q0:k1, pp * w:(pp + 1) * w]
        qa = []
        for h in range(2):
            c_hi, c_mid, c_lo = _split3_f32(cls[pp][h:h + 1, q0:q0 + 1])
            extra = jnp.where((lane == h) | (lane == 2 + h) | (lane == 4 + h), 1.0, 0.0)
            extra = extra + jnp.where(lane == 6, c_hi, 0.0) + jnp.where(lane == 7, c_mid, 0.0) \
                + jnp.where(lane == 8, c_lo, 0.0)
            qa.append(jnp.concatenate([_one_head(q2, first, h),
                                       jnp.broadcast_to(extra.astype(BF16), (tq, w))], axis=1))
        z = _dot(jnp.concatenate(qa, axis=0), kb[pp, :, 0:k1])
        zd = jnp.where(causal, z[:, q0:], -jnp.inf)
        return jnp.concatenate([z[:, :q0], zd], axis=1) if q0 > 0 else zd

    def finish(pp, q0, z):
        k1 = q0 + tq
        m = jnp.max(z, axis=-1, keepdims=True)
        p = jnp.exp2(z - m)
        l = jnp.sum(p, axis=-1, keepdims=True)
        o = _nt(p.astype(BF16), vb[pp, :, 0:k1]) / l
        o_ref[0, q0:k1, pp * w:(pp + 1) * w] = jnp.where(first, o[:tq], o[tq:]).astype(BF16)

    starts = list(range(0, t, tq))
    order = [(pp, q0) for q0 in starts[1::2] + starts[::2][::-1] for pp in range(pairs)]
    z = scores(*order[0])
    for n, item in enumerate(order):
        z_next = scores(*order[n + 1]) if n + 1 < len(order) else None
        finish(*item, z)
        z = z_next


def _attention_prompt(q, kt, vt, c, *, slot, tq, hd, pairs):
    b, t, d = q.shape
    w = 2 * hd * pairs
    qspec = pl.BlockSpec((1, t, w), lambda i, j: (i, 0, j))
    kspec = pl.BlockSpec((1, 1, w, t), lambda i, j: (slot, i, j, 0))
    cspec = pl.BlockSpec((1, pairs, 2, t), lambda i, j: (i, j, 0, 0))
    return pl.pallas_call(
        functools.partial(_attn_prompt_body, tq=tq, hd=hd),
        grid=(b, d // w),
        in_specs=[qspec, kspec, kspec, cspec],
        out_specs=qspec,
        out_shape=jax.ShapeDtypeStruct((b, t, d), BF16),
        scratch_shapes=[pltpu.VMEM((pairs, 4 * hd, t), BF16), pltpu.VMEM((pairs, 2 * hd, t), BF16)],
        compiler_params=_params(2),
        name="fox_attention_prompt",
    )(q, kt, vt, c)


def _attn_sample_body(q_ref, kc_ref, vc_ref, kn_ref, vn_ref, cc_ref, cn_ref, o_ref, *, hd):
    ts = q_ref.shape[1]
    w = 2 * hd
    first = _head_select(hd)
    row = lax.broadcasted_iota(jnp.int32, (2 * ts, ts), 0)
    causal = lax.broadcasted_iota(jnp.int32, (2 * ts, ts), 1) <= jnp.where(row >= ts, row - ts, row)
    for pp in range(q_ref.shape[2] // w):
        ls = slice(pp * w, (pp + 1) * w)
        q2 = q_ref[0, :, ls]
        kc = kc_ref[0, 0, ls, :].astype(BF16)
        vc = vc_ref[0, 0, ls, :].astype(BF16)
        kn = kn_ref[0, :, ls].astype(BF16)
        vn = vn_ref[0, :, ls].astype(BF16)
        qs = jnp.concatenate([_one_head(q2, first, 0), _one_head(q2, first, 1)], axis=0)
        cn = cn_ref[0, pp]

        def bias(c):
            b = (cn[:, 0:1] - c) * LOG2E
            return jnp.concatenate([jnp.broadcast_to(b[0:1], (ts, b.shape[1])),
                                    jnp.broadcast_to(b[1:2], (ts, b.shape[1]))], axis=0)

        zc = _dot(qs, kc) + bias(cc_ref[0, pp])
        zn = jnp.where(causal, _nt(qs, kn) + bias(cn), -jnp.inf)
        m = jnp.maximum(jnp.max(zc, axis=-1, keepdims=True), jnp.max(zn, axis=-1, keepdims=True))
        pc = jnp.exp2(zc - m)
        pn = jnp.exp2(zn - m)
        l = jnp.sum(pc, axis=-1, keepdims=True) + jnp.sum(pn, axis=-1, keepdims=True)
        o = (_nt(pc.astype(BF16), vc) + _dot(pn.astype(BF16), vn)) / l
        o_ref[0, :, ls] = jnp.where(first, o[:ts], o[ts:]).astype(BF16)


def _attention_sample(q, kc, vc, kn, vn, cc, cn, *, slot, hd, pairs):
    b, ts, d = q.shape
    past = kc.shape[3]
    w = 2 * hd * pairs
    qspec = pl.BlockSpec((1, ts, w), lambda i, j: (i, 0, j))
    kspec = pl.BlockSpec((1, 1, w, past), lambda i, j: (slot, i, j, 0))
    return pl.pallas_call(
        functools.partial(_attn_sample_body, hd=hd),
        grid=(b, d // w),
        in_specs=[qspec, kspec, kspec, qspec, qspec,
                  pl.BlockSpec((1, pairs, 2, past), lambda i, j: (i, j, 0, 0)),
                  pl.BlockSpec((1, pairs, 2, ts), lambda i, j: (i, j, 0, 0))],
        out_specs=qspec,
        out_shape=jax.ShapeDtypeStruct((b, ts, d), BF16),
        compiler_params=_params(2),
        name="fox_attention_sample",
    )(q, kc, vc, kn, vn, cc, cn)


def _hgrn_body(x_ref, w_ref, lb_ref, ng_ref, s0_ref, o_ref, sT_ref,
               state, state0, q_s, k_s, b_s, v_s, g_s, *, layer, nh, dk, chunk):
    tt, d = x_ref.shape[1], x_ref.shape[2]
    ti = pl.program_id(1)

    @pl.when(ti == 0)
    def _():
        state[...] = s0_ref[0]

    xb = x_ref[0].astype(BF16)
    lbp = lb_ref[...]
    e = jnp.exp(lbp - jnp.max(lbp, axis=0, keepdims=True))
    lb = jnp.sum(e[1:layer + 1, :], axis=0, keepdims=True) / jnp.sum(e, axis=0, keepdims=True)
    fx = _dot(xb, w_ref[:, d:2 * d])
    q_s[...] = _silu(_dot(xb, w_ref[:, 0:d]))
    v = _dot(xb, w_ref[:, 2 * d:3 * d])
    v_s[...] = v
    g_s[...] = _silu(_dot(xb, w_ref[:, 3 * d:4 * d]))
    k_s[...] = (1.0 - lb) * jax.nn.sigmoid(-fx)
    g = jnp.log(lb + (1.0 - lb) * jax.nn.sigmoid(fx))
    tri = (lax.broadcasted_iota(jnp.int32, (chunk, chunk), 0)
           >= lax.broadcasted_iota(jnp.int32, (chunk, chunk), 1)).astype(BF16)
    tri3 = jnp.concatenate([tri, tri, tri], axis=1)
    mid = chunk // 2 - 1
    ng = ng_ref[...]
    vb = v.astype(BF16)
    state0[...] = state[...]
    causal = (lax.broadcasted_iota(jnp.int32, (chunk, chunk), 0)
              >= lax.broadcasted_iota(jnp.int32, (chunk, chunk), 1))

    def finish(h, r0, rows, o, st, bl, upd):
        ls = slice(h * dk, (h + 1) * dk)
        state[h] = jnp.exp(bl) * st + upd
        ms = jnp.mean(o * o, axis=-1, keepdims=True)
        on = o * lax.rsqrt(ms + RMS_EPS) * ng * g_s[pl.ds(r0, rows), ls]
        o_ref[0, pl.ds(r0, rows), ls] = on.astype(BF16)

    spread = jnp.zeros((1, 1), F32)
    for c in range(tt // chunk):
        r0 = c * chunk
        rows = slice(r0, r0 + chunk)
        bc = _dot(tri3, jnp.concatenate(list(_split3(g[rows])), axis=0))
        b_s[rows, :] = bc
        bm = bc[mid:mid + 1, :]
        bl = bc[chunk - 1:chunk, :]
        dd = bc - bm
        spread = jnp.maximum(spread, jnp.max(jnp.abs(dd), keepdims=True))
        qt = (q_s[rows, :] * jnp.exp(dd)).astype(BF16)
        kh = k_s[rows, :] * jnp.exp(-dd)
        kt = (kh * jnp.exp(bl - bm)).astype(BF16)
        kh = kh.astype(BF16)
        ebm = jnp.exp(bm)
        for h in range(nh):
            ls = slice(h * dk, (h + 1) * dk)
            st = state[h]
            dv = st.shape[0]
            rhs = jnp.concatenate([(st * ebm[:, ls]).astype(BF16), kh[:, ls]], axis=0)
            both = _nt(qt[:, ls], rhs)
            sc = jnp.where(causal, both[:, dv:], 0.0).astype(BF16)
            o = _dot(sc, vb[rows, ls]) + both[:, :dv]
            finish(h, r0, chunk, o, st, bl[:, ls], _tn(vb[rows, ls], kt[:, ls]))
    factorable = spread[0, 0] <= HGRN_SAFE_EXPONENT

    @pl.when(jnp.logical_not(factorable))
    def _():
        state[...] = state0[...]
        sub = min(HGRN_SUB, chunk)
        row = lax.broadcasted_iota(jnp.int32, (sub, 1), 0)

        def step(i, carry):
            r0 = pl.multiple_of(i * sub, sub)
            inside = (r0 % chunk) != 0
            prev = pl.multiple_of(jnp.maximum(r0 - 8, 0), 8)
            for h in range(nh):
                ls = slice(h * dk, (h + 1) * dk)
                qs = q_s[pl.ds(r0, sub), ls]
                ks = k_s[pl.ds(r0, sub), ls]
                vs = v_s[pl.ds(r0, sub), ls]
                base = jnp.where(inside, b_s[pl.ds(prev, 8), ls][7:8, :], 0.0)
                bs = b_s[pl.ds(r0, sub), ls] - base
                bl = bs[sub - 1:sub, :]
                od = jnp.zeros((sub, dk), F32)
                for s in range(sub):
                    dec = jnp.exp(jnp.where(row >= s, bs - bs[s:s + 1, :], -jnp.inf))
                    r = jnp.sum(qs * ks[s:s + 1, :] * dec, axis=-1, keepdims=True)
                    od = od + r * vs[s:s + 1, :]
                st = state[h]
                qt = (qs * jnp.exp(bs)).astype(BF16)
                kt = (ks * jnp.exp(bl - bs)).astype(BF16)
                o = od + _nt(qt, st.astype(BF16))
                finish(h, r0, sub, o, st, bl, _tn(vs.astype(BF16), kt))
            return carry

        lax.fori_loop(0, tt // sub, step, 0)

    @pl.when(ti == pl.num_programs(1) - 1)
    def _():
        sT_ref[0] = state[...]


def _hgrn(x, w, lb, ng, s0_t, *, layer, tt):
    bsz, t, d = x.shape
    nh, dv, dk = s0_t.shape[1:]
    chunk = min(HGRN_CHUNK, tt)
    blk = pl.BlockSpec((1, tt, d), lambda i, j: (i, j, 0))
    sspec = pl.BlockSpec((1, nh, dv, dk), lambda i, j: (i, 0, 0, 0))
    return pl.pallas_call(
        functools.partial(_hgrn_body, layer=layer, nh=nh, dk=dk, chunk=chunk),
        grid=(bsz, t // tt),
        in_specs=[blk, _const_spec(w.shape), _const_spec(lb.shape), _const_spec(ng.shape), sspec],
        out_specs=[blk, sspec],
        out_shape=[jax.ShapeDtypeStruct((bsz, t, d), BF16),
                   jax.ShapeDtypeStruct((bsz, nh, dv, dk), F32)],
        scratch_shapes=[pltpu.VMEM((nh, dv, dk), F32)] * 2 + [pltpu.VMEM((tt, d), F32)] * 5,
        compiler_params=_params(2), name="hgrn_mix",
    )(x, w, lb, ng, s0_t)


def _pool_body(x_ref, halo_ref, hist_ref, w_ref, sc_ref, y_ref, ext, *, pos0, hpad):
    tm = x_ref.shape[1]
    d = x_ref.shape[2]
    ti = pl.program_id(1)
    gw = d // len(POOL_WINDOWS)

    @pl.when(ti == 0)
    def _():
        ext[0:hpad, :] = hist_ref[0]

    @pl.when(ti > 0)
    def _():
        ext[0:hpad, :] = halo_ref[0]

    ext[hpad:hpad + tm, :] = x_ref[0]
    pos = pos0 + ti * tm + lax.broadcasted_iota(jnp.int32, (tm, 1), 0)
    ys = []
    for gi, w in enumerate(POOL_WINDOWS):
        ls = slice(gi * gw, (gi + 1) * gw)
        acc = ext[:, ls]
        xg = acc[hpad:, :]
        sh = 1
        while sh < w:
            acc = acc + pltpu.roll(acc, sh, axis=0)
            sh *= 2
        count = jnp.minimum(pos + 1, w).astype(F32)
        z = acc[hpad:, :] / count - xg
        ys.append(_dot(z.astype(BF16), w_ref[gi]))
    y_ref[0] = jnp.concatenate(ys, axis=1) * sc_ref[...]


def _pool(x, hist, w, scale, *, pos0, tm):
    bsz, t, d = x.shape
    hpad = hist.shape[1]
    blk = pl.BlockSpec((1, tm, d), lambda i, j: (i, j, 0))
    per = tm // hpad
    halo = pl.BlockSpec((1, hpad, d), lambda i, j: (i, jnp.maximum(j * per - 1, 0), 0))
    return pl.pallas_call(
        functools.partial(_pool_body, pos0=pos0, hpad=hpad),
        grid=(bsz, t // tm),
        in_specs=[blk, halo, pl.BlockSpec((1, hpad, d), lambda i, j: (i, 0, 0)),
                  pl.BlockSpec(w.shape, lambda i, j: (0, 0, 0)),
                  pl.BlockSpec(scale.shape, lambda i, j: (0, 0))],
        out_specs=blk,
        out_shape=jax.ShapeDtypeStruct((bsz, t, d), F32),
        scratch_shapes=[pltpu.VMEM((hpad + tm, d), F32)],
        compiler_params=_params(2), name="pool_mix",
    )(x, x, hist, w, scale)


def kernel(x_prompt, x_sample, cache_fox_k, cache_fox_v, cache_fox_lf, state_hgrn, cache_pool,
           w_fox_in, b_fox_f, w_fox_out, w_hgrn_in, hgrn_lb, hgrn_norm_g, w_hgrn_out,
           w_pool, pool_scale, w_ffn_in, w_ffn_out, ln_g, ln_b):
    bp, tp, d = x_prompt.shape
    bs, ts, _ = x_sample.shape
    depth = w_ffn_in.shape[0]
    nh = b_fox_f.shape[1]
    hd = d // nh
    past = cache_fox_k.shape[2]
    hist = cache_pool.shape[2]
    hg_heads, hg_dk, hg_dv = state_hgrn.shape[2:]
    alpha = (2 * depth) ** 0.25
    np_, ns = bp * tp, bs * ts
    tm_p = 512
    tq = 256
    hpad = 16

    xp = x_prompt.reshape(np_, d)
    xs = x_sample.reshape(ns, d)
    outs = {name: [] for name in ("hs_p", "pl_p", "fk_s", "fv_s", "flf_s", "hs_s", "pl_s")}
    n_fox = w_fox_in.shape[0]
    q_scale = LOG2E * hd ** -0.5
    fox_prompt = None
    cache_kt = jnp.transpose(cache_fox_k, (0, 1, 3, 4, 2)).reshape(n_fox, bs, d, past)
    cache_vt = jnp.transpose(cache_fox_v, (0, 1, 3, 4, 2)).reshape(n_fox, bs, d, past)
    cache_lft = jnp.swapaxes(cache_fox_lf, 2, 3)
    wi = w_ffn_in.astype(BF16)
    w2 = w_ffn_out.astype(BF16)
    w_fox_out_b = w_fox_out.astype(BF16)
    w_hgrn_out_b = w_hgrn_out.astype(BF16)

    for i in range(depth):
        j = i // N_MIXERS
        kind = i % N_MIXERS
        wo = None
        if kind == 0:
            w_in = w_fox_in[j]
            wq = w_in[:, :d].astype(BF16)
            wk = w_in[:, d:2 * d].astype(BF16)
            wv = w_in[:, 2 * d:3 * d].astype(BF16)
            wf = jnp.pad(w_in[:, 3 * d:], ((0, 0), (0, LANES - nh))).astype(BF16)
            wkvf = jnp.swapaxes(jnp.pad(w_in[:, d:], ((0, 0), (0, LANES - nh))), 0, 1).astype(BF16)
            bf = jnp.pad(b_fox_f[j], (0, LANES - nh)).reshape(1, LANES)
            wo = w_fox_out_b

            q, kt, vt, lft = _fox_proj_t(xp.reshape(bp, tp, d), wq, wkvf, b_fox_f[j].reshape(nh, 1),
                                         fox_prompt, layer_slot=j, n_slots=n_fox, q_scale=q_scale, tm=tm_p)
            fox_prompt = (kt, vt, lft)
            c, = _cumsum_time(lft, j)
            mp = _attention_prompt(q, kt, vt, c.reshape(bp, nh // 2, 2, tp), slot=j, tq=tq, hd=hd,
                                   pairs=ATTN_PAIRS)
            mp = mp.reshape(np_, d)

            q, k, v, lf = _fox_proj(xs, wq, wk, wv, wf, bf, nh=nh, q_scale=q_scale, tm=ns)
            cc, cn = _cumsum_time(cache_lft, j, jnp.swapaxes(lf.reshape(bs, ts, nh), 1, 2))
            ms = _attention_sample(q.reshape(bs, ts, d), cache_kt, cache_vt,
                                   k.reshape(bs, ts, d), v.reshape(bs, ts, d),
                                   cc.reshape(bs, nh // 2, 2, past), cn.reshape(bs, nh // 2, 2, ts),
                                   slot=j, hd=hd, pairs=SAMPLE_ATTN_PAIRS)
            ms = ms.reshape(ns, d)
            outs["fk_s"].append(k.reshape(bs, ts, nh, hd))
            outs["fv_s"].append(v.reshape(bs, ts, nh, hd))
            outs["flf_s"].append(lf.reshape(bs, ts, nh))
        elif kind == 1:
            w_in = w_hgrn_in[j].astype(BF16)
            wo = w_hgrn_out_b
            ng = hgrn_norm_g[j].reshape(1, hg_dv)
            s0 = jnp.zeros((bp, hg_heads, hg_dv, hg_dk), F32)
            mp, st = _hgrn(xp.reshape(bp, tp, d), w_in, hgrn_lb, ng, s0, layer=i, tt=tm_p)
            mp = mp.reshape(np_, d)
            outs["hs_p"].append(jnp.swapaxes(st, 2, 3))
            ms, st = _hgrn(xs.reshape(bs, ts, d), w_in, hgrn_lb, ng,
                           jnp.swapaxes(state_hgrn[j], 2, 3), layer=i, tt=ts)
            ms = ms.reshape(ns, d)
            outs["hs_s"].append(jnp.swapaxes(st, 2, 3))
        else:
            wp = w_pool[j].astype(BF16)
            sc = pool_scale[j].reshape(1, d)
            xp3 = xp.reshape(bp, tp, d)
            xs3 = xs.reshape(bs, ts, d)
            hist_s = jnp.pad(cache_pool[j], ((0, 0), (hpad - hist, 0), (0, 0)))
            mp = None
            ms = _pool(xs3, hist_s, wp, sc, pos0=past, tm=ts).reshape(ns, d)
            outs["pl_p"].append(xp3[:, tp - hist:])
            outs["pl_s"].append(jnp.concatenate([cache_pool[j], xs3], axis=1)[:, ts:])
        if mp is None:
            xp = _post_pool(xp, tp, wp, sc, ln_g, ln_b, wi, w2, i, alpha=alpha, tm=POST_TILE, hpad=hpad)
        else:
            xp = _post(xp, mp, wo, j, ln_g, ln_b, wi, w2, i, alpha=alpha, tm=POST_TILE)
        xs = _post(xs, ms, wo, j, ln_g, ln_b, wi, w2, i, alpha=alpha, tm=ns)

    stack = lambda name: jnp.stack(outs[name])
    kt, vt, lft = fox_prompt
    time_major = lambda a: jnp.transpose(a.reshape(n_fox, bp, nh, hd, tp), (0, 1, 4, 2, 3))
    return (xp.reshape(bp, tp, d), xs.reshape(bs, ts, d),
            time_major(kt), time_major(vt), jnp.swapaxes(lft, 2, 3), stack("hs_p"), stack("pl_p"),
            stack("fk_s"), stack("fv_s"), stack("flf_s"), stack("hs_s"), stack("pl_s"))
```
